```python
import math
import jax, jax.numpy as jnp
from jax import lax
import numpy as np

D_MODEL = 1024
BATCH = 2
SEQ = 8192
DEPTH = 2
DEC_BATCH = 128
DEC_SEQ = 1
PAST_LEN = 2048
PAGE_SIZE = 128

F32 = jnp.float32
MIX_WIDTH = D_MODEL
ATT_WIDTH = MIX_WIDTH // 2
RNN_WIDTH = MIX_WIDTH - ATT_WIDTH
N_ATT_HEADS = 4
V_DIM = ATT_WIDTH // N_ATT_HEADS
QK_DIM = V_DIM // 2
ROT_DIM = QK_DIM // 4
ROPE_THETA = 500000.0
N_RNN_HEADS = 4
RNN_K_DIM = RNN_WIDTH // N_RNN_HEADS
RNN_V_DIM = RNN_WIDTH // N_RNN_HEADS
RNN_CHUNK = 64
Q_BLOCK = 128
Q_ATT = N_ATT_HEADS * 2 * QK_DIM
IN_WIDTH = 2 * Q_ATT + ATT_WIDTH + 4 * RNN_WIDTH
D_FF = 2816
N_EXPERTS = 8
TOP_K = 2
D_FF_EXPERT = 3584
N_DENSE = (DEPTH + 1) // 2
N_MOE = DEPTH // 2
EPS = 1e-6
NEG = -1e30

kernel_name = 'hymba_diffattn_hgrn2_moe_step'


def rmsnorm(x, g):
    xf = x.astype(F32)
    r = lax.rsqrt(jnp.mean(xf * xf, axis=-1, keepdims=True) + EPS)
    return (xf * r * g.astype(F32)).astype(x.dtype)


def rope(x, pos):
    half = ROT_DIM // 2
    inv = jnp.power(ROPE_THETA, -jnp.arange(half, dtype=F32) * 2.0 / ROT_DIM)
    ang = pos.astype(F32)[:, None] * inv[None, :]
    cos = jnp.cos(ang)[None, :, None, None, :]
    sin = jnp.sin(ang)[None, :, None, None, :]
    xf = x.astype(F32)
    x1, x2, rest = xf[..., :half], xf[..., half:ROT_DIM], xf[..., ROT_DIM:]
    out = jnp.concatenate([x1 * cos - x2 * sin, x2 * cos + x1 * sin, rest], axis=-1)
    return out.astype(x.dtype)


def diff_attn_core(q, k, v, q_pos, k_pos, lam):
    s = jnp.einsum('bqhcd,bkhcd->bhcqk', q.astype(F32), k.astype(F32)) * (QK_DIM ** -0.5)
    mask = k_pos[None, :] <= q_pos[:, None]
    p = jax.nn.softmax(jnp.where(mask, s, NEG), axis=-1)
    a = p[:, :, 0] - lam * p[:, :, 1]
    return jnp.einsum('bhqk,bkhd->bqhd', a, v.astype(F32))


def diff_attention(q, k, v, q_pos, k_pos, lam):
    B, T = q.shape[0], q.shape[1]
    if T > Q_BLOCK and T % Q_BLOCK == 0:
        nb = T // Q_BLOCK
        qb = jnp.moveaxis(q.reshape(B, nb, Q_BLOCK, N_ATT_HEADS, 2, QK_DIM), 1, 0)
        pb = q_pos.reshape(nb, Q_BLOCK)
        ob = lax.map(lambda a: diff_attn_core(a[0], k, v, a[1], k_pos, lam), (qb, pb))
        return jnp.moveaxis(ob, 0, 1).reshape(B, T, N_ATT_HEADS, V_DIM)
    return diff_attn_core(q, k, v, q_pos, k_pos, lam)


def hgrn2_recurrence(q, k, v, logf, s0):
    B, T = q.shape[0], q.shape[1]
    C = RNN_CHUNK if T >= RNN_CHUNK else T
    pad = (-T) % C
    v = v.astype(F32)
    if pad:
        pw = ((0, 0), (0, pad), (0, 0), (0, 0))
        q, k, v, logf = jnp.pad(q, pw), jnp.pad(k, pw), jnp.pad(v, pw), jnp.pad(logf, pw)
    n = (T + pad) // C
    chunks = lambda a: jnp.swapaxes(a.reshape(B, n, C, a.shape[2], a.shape[3]), 0, 1)
    tri = jnp.tril(jnp.ones((C, C), dtype=bool))[None, :, :, None, None]

    def step(S, inp):
        qc, kc, vc, gc = inp
        b = jnp.cumsum(gc, axis=1)
        o_inter = jnp.einsum('bthk,bhkv->bthv', qc * jnp.exp(b), S)
        diff = b[:, :, None] - b[:, None, :]
        decay = jnp.where(tri, jnp.exp(jnp.where(tri, diff, 0.0)), 0.0)
        A = jnp.einsum('btshk,bshk->bhts', qc[:, :, None] * decay, kc)
        o_intra = jnp.einsum('bhts,bshv->bthv', A, vc)
        bl = b[:, -1]
        S = S * jnp.exp(bl)[..., None] + jnp.einsum('bshk,bshv->bhkv', kc * jnp.exp(bl[:, None] - b), vc)
        return S, o_inter + o_intra

    S, o = lax.scan(step, s0.astype(F32), (chunks(q), chunks(k), chunks(v), chunks(logf)))
    o = jnp.swapaxes(o, 0, 1).reshape(B, n * C, N_RNN_HEADS, RNN_V_DIM)[:, :T]
    return o, S


def mixer_inputs(x, ln_g, w_in, qn_g, kn_g, lb, pos):
    B, T, _ = x.shape
    z = rmsnorm(x, ln_g) @ w_in
    offs = [Q_ATT, 2 * Q_ATT, 2 * Q_ATT + ATT_WIDTH, 2 * Q_ATT + ATT_WIDTH + RNN_WIDTH,
            2 * Q_ATT + ATT_WIDTH + 2 * RNN_WIDTH, 2 * Q_ATT + ATT_WIDTH + 3 * RNN_WIDTH]
    qa, ka, va, qr, fr, ir, gr = jnp.split(z, offs, axis=-1)
    qa = rope(rmsnorm(qa.reshape(B, T, N_ATT_HEADS, 2, QK_DIM), qn_g), pos)
    ka = rope(rmsnorm(ka.reshape(B, T, N_ATT_HEADS, 2, QK_DIM), kn_g), pos)
    va = va.reshape(B, T, N_ATT_HEADS, V_DIM)
    heads = lambda a: a.reshape(B, T, N_RNN_HEADS, RNN_K_DIM).astype(F32)
    fx = heads(fr)
    lbh = lb.reshape(N_RNN_HEADS, RNN_K_DIM).astype(F32)
    logf = jnp.logaddexp(jnp.log(lbh), jnp.log1p(-lbh) + jax.nn.log_sigmoid(fx))
    kr = (1.0 - lbh) * jax.nn.sigmoid(-fx)
    qr = jax.nn.silu(heads(qr))
    vr = ir.reshape(B, T, N_RNN_HEADS, RNN_V_DIM)
    return qa, ka, va, qr, kr, vr, logf, gr


def mixer_output(x, att_o, subln_g, lam_init, rnn_o, rnn_g, gate, w_out):
    B, T, _ = x.shape
    att = rmsnorm(att_o, subln_g) * (1.0 - lam_init)
    g = jax.nn.silu(gate.reshape(B, T, N_RNN_HEADS, RNN_V_DIM).astype(F32))
    rnn = rmsnorm(rnn_o, rnn_g) * g
    mix = jnp.concatenate([att.reshape(B, T, ATT_WIDTH), rnn.reshape(B, T, RNN_WIDTH)], axis=-1)
    return x + (mix.astype(x.dtype) @ w_out)


def swiglu(h, w_gu, w_dn):
    g, u = jnp.split(h @ w_gu, 2, axis=-1)
    return (jax.nn.silu(g) * u) @ w_dn


def moe_swiglu(h, w_router, w_gu, w_dn):
    logits = jnp.einsum('btd,de->bte', h.astype(F32), w_router.astype(F32))
    top_v, top_i = lax.top_k(logits, TOP_K)
    gates = jax.nn.softmax(top_v, axis=-1)
    comb = jnp.einsum('btk,btke->bte', gates, jax.nn.one_hot(top_i, N_EXPERTS, dtype=F32))
    out = jnp.zeros(h.shape, F32)
    for e in range(N_EXPERTS):
        out = out + comb[..., e:e + 1] * swiglu(h, w_gu[e], w_dn[e]).astype(F32)
    return out.astype(h.dtype)


def channel_mixer(x, l, ln2_g, w_ffn_gu, w_ffn_dn, w_router, w_exp_gu, w_exp_dn):
    h = rmsnorm(x, ln2_g[l])
    if l % 2 == 0:
        return x + swiglu(h, w_ffn_gu[l // 2], w_ffn_dn[l // 2])
    return x + moe_swiglu(h, w_router[l // 2], w_exp_gu[l // 2], w_exp_dn[l // 2])


def setup_inputs(seed: int = 0) -> dict:
    key = jax.random.key(seed)
    ks = jax.random.split(key, 24)
    nrm = lambda k, s: jax.random.normal(k, s, dtype=F32)
    n_pages = PAST_LEN // PAGE_SIZE
    n_used = DEC_BATCH * n_pages
    n_pool = n_used + max(1, n_used // 4)
    page_table = jax.random.permutation(ks[0], n_pool)[:n_used].reshape(DEC_BATCH, n_pages).astype(jnp.int32)
    return {
        'x_prompt': nrm(ks[1], (BATCH, SEQ, D_MODEL)),
        'x_sample': nrm(ks[2], (DEC_BATCH, DEC_SEQ, D_MODEL)),
        'cache_k': nrm(ks[3], (DEPTH, n_pool, PAGE_SIZE, N_ATT_HEADS, 2 * QK_DIM)),
        'cache_v': nrm(ks[4], (DEPTH, n_pool, PAGE_SIZE, N_ATT_HEADS, V_DIM)),
        'state_hgrn': 0.3 * nrm(ks[5], (DEPTH, DEC_BATCH, N_RNN_HEADS, RNN_K_DIM, RNN_V_DIM)),
        'page_table': page_table,
        'ln1_g': 1.0 + 0.02 * nrm(ks[6], (DEPTH, D_MODEL)),
        'w_in': nrm(ks[7], (DEPTH, D_MODEL, IN_WIDTH)) * D_MODEL ** -0.5,
        'q_norm_g': 1.0 + 0.02 * nrm(ks[8], (DEPTH, 2, QK_DIM)),
        'k_norm_g': 1.0 + 0.02 * nrm(ks[9], (DEPTH, 2, QK_DIM)),
        'lam_p': 0.1 * nrm(ks[10], (DEPTH, 4, QK_DIM)),
        'subln_g': 1.0 + 0.02 * nrm(ks[11], (DEPTH, V_DIM)),
        'hgrn_lb': 0.5 * nrm(ks[12], (DEPTH, RNN_WIDTH)),
        'hgrn_norm_g': 1.0 + 0.02 * nrm(ks[13], (DEPTH, RNN_V_DIM)),
        'w_out': nrm(ks[14], (DEPTH, MIX_WIDTH, D_MODEL)) * MIX_WIDTH ** -0.5,
        'ln2_g': 1.0 + 0.02 * nrm(ks[15], (DEPTH, D_MODEL)),
        'w_ffn_gu': nrm(ks[16], (N_DENSE, D_MODEL, 2 * D_FF)) * D_MODEL ** -0.5,
        'w_ffn_dn': nrm(ks[17], (N_DENSE, D_FF, D_MODEL)) * D_FF ** -0.5,
        'w_router': nrm(ks[18], (N_MOE, D_MODEL, N_EXPERTS)) * D_MODEL ** -0.5,
        'w_exp_gu': nrm(ks[19], (N_MOE, N_EXPERTS, D_MODEL, 2 * D_FF_EXPERT)) * D_MODEL ** -0.5,
        'w_exp_dn': nrm(ks[20], (N_MOE, N_EXPERTS, D_FF_EXPERT, D_MODEL)) * D_FF_EXPERT ** -0.5,
    }


def reference(x_prompt, x_sample, cache_k, cache_v, state_hgrn, page_table, ln1_g, w_in,
              q_norm_g, k_norm_g, lam_p, subln_g, hgrn_lb, hgrn_norm_g, w_out, ln2_g,
              w_ffn_gu, w_ffn_dn, w_router, w_exp_gu, w_exp_dn):
    Bp, Tp, _ = x_prompt.shape
    Bs, Ts, _ = x_sample.shape
    past = page_table.shape[1] * cache_k.shape[2]
    pos_p = jnp.arange(Tp)
    pos_s = past + jnp.arange(Ts)
    kpos_s = jnp.arange(past + Ts)
    lbs = jnp.cumsum(jax.nn.softmax(hgrn_lb.astype(F32), axis=0), axis=0)
    lbs = lbs - lbs[0]
    xp, xs = x_prompt, x_sample
    kp, vp, sp, ks, vs, ss = [], [], [], [], [], []
    for l in range(DEPTH):
        lam_init = 0.8 - 0.6 * math.exp(-0.3 * l)
        lp = lam_p[l].astype(F32)
        lam = jnp.exp(jnp.sum(lp[0] * lp[1])) - jnp.exp(jnp.sum(lp[2] * lp[3])) + lam_init
        qa, ka, va, qr, kr, vr, lf, gr = mixer_inputs(xp, ln1_g[l], w_in[l], q_norm_g[l], k_norm_g[l], lbs[l], pos_p)
        att = diff_attention(qa, ka, va, pos_p, pos_p, lam)
        s0 = jnp.zeros((Bp, N_RNN_HEADS, RNN_K_DIM, RNN_V_DIM), F32)
        rnn, s_fin = hgrn2_recurrence(qr, kr, vr, lf, s0)
        xp = mixer_output(xp, att, subln_g[l], lam_init, rnn, hgrn_norm_g[l], gr, w_out[l])
        xp = channel_mixer(xp, l, ln2_g, w_ffn_gu, w_ffn_dn, w_router, w_exp_gu, w_exp_dn)
        kp.append(ka.reshape(Bp, Tp, N_ATT_HEADS, 2 * QK_DIM).astype(cache_k.dtype))
        vp.append(va.astype(cache_v.dtype))
        sp.append(s_fin.astype(state_hgrn.dtype))
        qa, ka, va, qr, kr, vr, lf, gr = mixer_inputs(xs, ln1_g[l], w_in[l], q_norm_g[l], k_norm_g[l], lbs[l], pos_s)
        k_past = cache_k[l][page_table].reshape(Bs, past, N_ATT_HEADS, 2, QK_DIM)
        v_past = cache_v[l][page_table].reshape(Bs, past, N_ATT_HEADS, V_DIM)
        k_all = jnp.concatenate([k_past, ka.astype(k_past.dtype)], axis=1)
        v_all = jnp.concatenate([v_past, va.astype(v_past.dtype)], axis=1)
        att = diff_attention(qa, k_all, v_all, pos_s, kpos_s, lam)
        rnn, s_new = hgrn2_recurrence(qr, kr, vr, lf, state_hgrn[l])
        xs = mixer_output(xs, att, subln_g[l], lam_init, rnn, hgrn_norm_g[l], gr, w_out[l])
        xs = channel_mixer(xs, l, ln2_g, w_ffn_gu, w_ffn_dn, w_router, w_exp_gu, w_exp_dn)
        ks.append(ka.reshape(Bs, Ts, N_ATT_HEADS, 2 * QK_DIM).astype(cache_k.dtype))
        vs.append(va.astype(cache_v.dtype))
        ss.append(s_new.astype(state_hgrn.dtype))
    return (xp, xs, jnp.stack(kp), jnp.stack(vp), jnp.stack(sp), jnp.stack(ks), jnp.stack(vs), jnp.stack(ss))
```

```python
import functools
import math

import jax
import jax.numpy as jnp
import numpy as np
from jax import lax
from jax.experimental import pallas as pl
from jax.experimental.pallas import tpu as pltpu

F32 = jnp.float32
BF16 = jnp.bfloat16

D_MODEL = 1024
N_ATT_HEADS = 4
V_DIM = 128
QK_DIM = 64
ROT_DIM = 16
ROPE_THETA = 500000.0
N_RNN_HEADS = 4
RNN_DIM = 128
ATT_WIDTH = N_ATT_HEADS * V_DIM
RNN_WIDTH = N_RNN_HEADS * RNN_DIM
IN_WIDTH = 3 * ATT_WIDTH + 4 * RNN_WIDTH
N_EXPERTS = 8
EPS = 1e-6
NEG = -1e30
LANES = 128
V7X_VMEM_BYTES = 64 * 1024 * 1024
HGRN_CHUNK = 128
HGRN_SUB = 16

NT_DIMS = (((1,), (1,)), ((), ()))


def _params(semantics, vmem_mb):
    return pltpu.CompilerParams(dimension_semantics=semantics,
                                vmem_limit_bytes=vmem_mb * 1024 * 1024)


def _sigmoid(x):
    return 1.0 / (1.0 + jnp.exp(-x))


def _mm(a, b, exact, dims=None):
    if exact:
        a, b, prec = a.astype(F32), b.astype(F32), lax.Precision.HIGHEST
    else:
        a, b, prec = a.astype(BF16), b.astype(BF16), None
    if dims is None:
        return jnp.dot(a, b, precision=prec, preferred_element_type=F32)
    return lax.dot_general(a, b, dims, precision=prec, preferred_element_type=F32)


def _split_bf16(x, parts):
    out = []
    for _ in range(parts - 1):
        h = x.astype(BF16)
        out.append(h)
        x = x - h.astype(F32)
    out.append(x.astype(BF16))
    return out


def _group_mean_sq(x, bd):
    parts = _split_bf16(x * x, 2)
    return (jnp.dot(parts[0], bd, preferred_element_type=F32)
            + jnp.dot(parts[1], bd, preferred_element_type=F32))


def _qk_norm_rope(z, gain, bd, cos, sin_hi, sin_lo):
    r = lax.rsqrt(_group_mean_sq(z, bd) + EPS)
    y = z * r * gain
    half = ROT_DIM // 2
    up = pltpu.roll(y, LANES - half, axis=1)
    dn = pltpu.roll(y, half, axis=1)
    return y * cos + up * sin_hi + dn * sin_lo


def _inproj_kernel(x_ref, g_ref, w_ref, qg_ref, kg_ref, bd_ref, cos_ref, shi_ref, slo_ref,
                   q_ref, k_ref, v_ref, r_ref, *, exact):
    x = x_ref[...]
    r = lax.rsqrt(jnp.mean(x * x, axis=-1, keepdims=True) + EPS)
    hn = x * r * g_ref[...]
    if not exact:
        hn = hn.astype(BF16)
    bd = bd_ref[...]
    cos, shi, slo = cos_ref[...], shi_ref[...], slo_ref[...]
    for h in range(N_ATT_HEADS):
        lo, hi = h * LANES, (h + 1) * LANES
        zq = _mm(hn, w_ref[:, lo:hi], exact)
        q_ref[:, lo:hi] = _qk_norm_rope(zq, qg_ref[...], bd, cos, shi, slo)
        zk = _mm(hn, w_ref[:, ATT_WIDTH + lo:ATT_WIDTH + hi], exact)
        k_ref[:, lo:hi] = _qk_norm_rope(zk, kg_ref[...], bd, cos, shi, slo)
    v_ref[...] = _mm(hn, w_ref[:, 2 * ATT_WIDTH:3 * ATT_WIDTH], exact)
    for j in range(4):
        lo, hi = j * RNN_WIDTH, (j + 1) * RNN_WIDTH
        r_ref[:, lo:hi] = _mm(hn, w_ref[:, 3 * ATT_WIDTH + lo:3 * ATT_WIDTH + hi], exact)


def _input_projection(x, ln_g, w, qg, kg, bd, tabs, tm):
    n = x.shape[0]
    cos, shi, slo = tabs
    nt = cos.shape[0] // tm
    row = lambda i: (i, 0)
    fixed = lambda i: (0, 0)
    tab = lambda i: (i % nt, 0)
    return pl.pallas_call(
        functools.partial(_inproj_kernel, exact=w.dtype == F32),
        grid=(n // tm,),
        in_specs=[
            pl.BlockSpec((tm, D_MODEL), row),
            pl.BlockSpec((1, D_MODEL), fixed),
            pl.BlockSpec((D_MODEL, IN_WIDTH), fixed),
            pl.BlockSpec((1, LANES), fixed),
            pl.BlockSpec((1, LANES), fixed),
            pl.BlockSpec((LANES, LANES), fixed),
            pl.BlockSpec((tm, LANES), tab),
            pl.BlockSpec((tm, LANES), tab),
            pl.BlockSpec((tm, LANES), tab),
        ],
        out_specs=[
            pl.BlockSpec((tm, ATT_WIDTH), row),
            pl.BlockSpec((tm, ATT_WIDTH), row),
            pl.BlockSpec((tm, ATT_WIDTH), row),
            pl.BlockSpec((tm, 4 * RNN_WIDTH), row),
        ],
        out_shape=[
            jax.ShapeDtypeStruct((n, ATT_WIDTH), F32),
            jax.ShapeDtypeStruct((n, ATT_WIDTH), F32),
            jax.ShapeDtypeStruct((n, ATT_WIDTH), F32),
            jax.ShapeDtypeStruct((n, 4 * RNN_WIDTH), F32),
        ],
        compiler_params=_params(("parallel",), 48),
        name="input_projection",
    )(x, ln_g, w, qg, kg, bd, cos, shi, slo)


def _rope_tables(start, t):
    half = ROT_DIM // 2
    inv = np.power(ROPE_THETA, -np.arange(half, dtype=np.float64) * 2.0 / ROT_DIM)
    pos = start + np.arange(t)
    coarse = np.arange(pos[0] // LANES, pos[-1] // LANES + 1, dtype=np.float64)[:, None] * LANES * inv
    fine = np.arange(LANES, dtype=np.float64)[:, None] * inv
    const = lambda a: jnp.asarray(a.astype(np.float32))
    a_idx = jnp.asarray(pos // LANES - pos[0] // LANES)
    b_idx = jnp.asarray(pos % LANES)
    ca, sa = const(np.cos(coarse))[a_idx], const(np.sin(coarse))[a_idx]
    cb, sb = const(np.cos(fine))[b_idx], const(np.sin(fine))[b_idx]
    cos, sin = ca * cb - sa * sb, sa * cb + ca * sb
    one = jnp.ones((t, QK_DIM - ROT_DIM), F32)
    zero = jnp.zeros((t, QK_DIM - ROT_DIM), F32)
    zh = jnp.zeros((t, half), F32)
    comp = lambda a, b, rest: jnp.concatenate([a, b, rest], axis=1)
    c64 = comp(cos, cos, one)
    hi64 = comp(-sin, zh, zero)
    lo64 = comp(zh, sin, zero)
    two = lambda a: jnp.concatenate([a, a], axis=1)
    return two(c64), two(hi64), two(lo64)


def _attn_prompt_kernel(lam_ref, q_ref, k_ref, v_ref, g_ref, o_ref, acc1, acc2, *, tq, lam_scale):
    qi = pl.program_id(2)
    q = q_ref[...] * (QK_DIM ** -0.5)
    q1 = q[:, :QK_DIM].astype(BF16)
    q2 = q[:, QK_DIM:].astype(BF16)
    acc1[...] = jnp.zeros_like(acc1)
    acc2[...] = jnp.zeros_like(acc2)

    def tile(j, carry, masked):
        m1, l1, m2, l2 = carry
        start = pl.multiple_of(j * tq, tq)
        kb = k_ref[pl.ds(start, tq), :].astype(BF16)
        vb = v_ref[pl.ds(start, tq), :].astype(BF16)
        s1 = lax.dot_general(q1, kb[:, :QK_DIM], NT_DIMS, preferred_element_type=F32)
        s2 = lax.dot_general(q2, kb[:, QK_DIM:], NT_DIMS, preferred_element_type=F32)
        if masked:
            rows = lax.broadcasted_iota(jnp.int32, (tq, tq), 0)
            cols = lax.broadcasted_iota(jnp.int32, (tq, tq), 1)
            keep = cols <= rows
            s1 = jnp.where(keep, s1, NEG)
            s2 = jnp.where(keep, s2, NEG)
        out = []
        for s, m, l, acc in ((s1, m1, l1, acc1), (s2, m2, l2, acc2)):
            m_new = jnp.maximum(m, jnp.max(s, axis=-1, keepdims=True))
            alpha = jnp.exp(m - m_new)
            p = jnp.exp(s - m_new)
            l_new = alpha * l + jnp.sum(p, axis=-1, keepdims=True)
            acc[...] = alpha * acc[...] + jnp.dot(p.astype(BF16), vb, preferred_element_type=F32)
            out += [m_new, l_new]
        return tuple(out)

    init = (jnp.full((tq, 1), NEG, F32), jnp.zeros((tq, 1), F32),
            jnp.full((tq, 1), NEG, F32), jnp.zeros((tq, 1), F32))
    carry = lax.fori_loop(0, qi, lambda j, c: tile(j, c, False), init)
    _, l1, _, l2 = tile(qi, carry, True)
    o = acc1[...] / l1 - lam_ref[0] * (acc2[...] / l2)
    r = lax.rsqrt(jnp.mean(o * o, axis=-1, keepdims=True) + EPS)
    o_ref[...] = o * r * g_ref[...] * lam_scale


def _attention_prompt(q, k, v, lam, subln_g, lam_init, batch, seq, tq):
    nq = seq // tq
    kernel = functools.partial(_attn_prompt_kernel, tq=tq, lam_scale=1.0 - lam_init)
    return pl.pallas_call(
        kernel,
        grid=(batch, N_ATT_HEADS, nq),
        in_specs=[
            pl.BlockSpec(memory_space=pltpu.SMEM),
            pl.BlockSpec((tq, V_DIM), lambda b, h, i: (b * nq + i, h)),
            pl.BlockSpec((seq, V_DIM), lambda b, h, i: (b, h)),
            pl.BlockSpec((seq, V_DIM), lambda b, h, i: (b, h)),
            pl.BlockSpec((1, V_DIM), lambda b, h, i: (0, 0)),
        ],
        out_specs=pl.BlockSpec((tq, V_DIM), lambda b, h, i: (b * nq + i, h)),
        out_shape=jax.ShapeDtypeStruct((batch * seq, ATT_WIDTH), F32),
        scratch_shapes=[pltpu.VMEM((tq, V_DIM), F32), pltpu.VMEM((tq, V_DIM), F32)],
        compiler_params=_params(("parallel", "parallel", "arbitrary"), 48),
        name="attention_prompt",
    )(lam, q, k, v, subln_g)


def _attn_decode_kernel(pt_ref, lam_ref, q_ref, kn_ref, vn_ref, kp_ref, vp_ref, g_ref, o_ref,
                        m_scr, l_scr, acc_scr, *, n_pages, lam_scale):
    p_idx = pl.program_id(1)
    rows_per_page = kp_ref.shape[0]
    lane = lax.broadcasted_iota(jnp.int32, (N_ATT_HEADS, V_DIM), 1)
    q4 = q_ref[0] * (QK_DIM ** -0.5)
    qm = jnp.concatenate([jnp.where(lane < QK_DIM, q4, 0.0), jnp.where(lane >= QK_DIM, q4, 0.0)], axis=0)

    @pl.when(p_idx == 0)
    def _():
        m_scr[...] = jnp.full_like(m_scr, NEG)
        l_scr[...] = jnp.zeros_like(l_scr)
        acc_scr[...] = jnp.zeros_like(acc_scr)

    s = _mm(qm, kp_ref[...], True, NT_DIMS)
    r_i = lax.broadcasted_iota(jnp.int32, (2 * N_ATT_HEADS, rows_per_page), 0)
    c_i = lax.broadcasted_iota(jnp.int32, (2 * N_ATT_HEADS, rows_per_page), 1)
    keep = (c_i % N_ATT_HEADS) == (r_i % N_ATT_HEADS)
    s = jnp.where(keep, s, NEG)
    m_old = m_scr[...]
    m_new = jnp.maximum(m_old, jnp.max(s, axis=-1, keepdims=True))
    alpha = jnp.exp(m_old - m_new)
    p = jnp.where(keep, jnp.exp(s - m_new), 0.0)
    l_new = alpha * l_scr[...] + jnp.sum(p, axis=-1, keepdims=True)
    acc_new = alpha * acc_scr[...] + _mm(p, vp_ref[...], True)
    m_scr[...] = m_new
    l_scr[...] = l_new
    acc_scr[...] = acc_new

    @pl.when(p_idx == n_pages - 1)
    def _():
        prod = q4 * kn_ref[0]
        s_a = jnp.sum(jnp.where(lane < QK_DIM, prod, 0.0), axis=-1, keepdims=True)
        s_b = jnp.sum(jnp.where(lane >= QK_DIM, prod, 0.0), axis=-1, keepdims=True)
        s_n = jnp.concatenate([s_a, s_b], axis=0)
        m_fin = jnp.maximum(m_new, s_n)
        a_fin = jnp.exp(m_new - m_fin)
        p_n = jnp.exp(s_n - m_fin)
        l_fin = a_fin * l_new + p_n
        vn = vn_ref[0]
        acc_fin = a_fin * acc_new + p_n * jnp.concatenate([vn, vn], axis=0)
        o8 = acc_fin / l_fin
        o = o8[:N_ATT_HEADS] - lam_ref[0] * o8[N_ATT_HEADS:]
        r = lax.rsqrt(jnp.mean(o * o, axis=-1, keepdims=True) + EPS)
        o_ref[0] = o * r * g_ref[...] * lam_scale


def _attention_decode(q, k_new, v_new, cache_k, cache_v, page_table, layer, lam, subln_g, lam_init):
    nb = q.shape[0]
    n_pages = page_table.shape[1]
    rows = cache_k.shape[2]
    heads3 = lambda a: a.reshape(nb, N_ATT_HEADS, V_DIM)
    tok = lambda b, p, pt: (b, 0, 0)
    page = lambda b, p, pt: (layer, pt[b * n_pages + p], 0, 0)
    kernel = functools.partial(_attn_decode_kernel, n_pages=n_pages, lam_scale=1.0 - lam_init)
    out = pl.pallas_call(
        kernel,
        grid_spec=pltpu.PrefetchScalarGridSpec(
            num_scalar_prefetch=1,
            grid=(nb, n_pages),
            in_specs=[
                pl.BlockSpec(memory_space=pltpu.SMEM),
                pl.BlockSpec((1, N_ATT_HEADS, V_DIM), tok),
                pl.BlockSpec((1, N_ATT_HEADS, V_DIM), tok),
                pl.BlockSpec((1, N_ATT_HEADS, V_DIM), tok),
                pl.BlockSpec((None, None, rows, V_DIM), page),
                pl.BlockSpec((None, None, rows, V_DIM), page),
                pl.BlockSpec((1, V_DIM), lambda b, p, pt: (0, 0)),
            ],
            out_specs=pl.BlockSpec((1, N_ATT_HEADS, V_DIM), tok),
            scratch_shapes=[
                pltpu.VMEM((2 * N_ATT_HEADS, 1), F32),
                pltpu.VMEM((2 * N_ATT_HEADS, 1), F32),
                pltpu.VMEM((2 * N_ATT_HEADS, V_DIM), F32),
            ],
        ),
        out_shape=jax.ShapeDtypeStruct((nb, N_ATT_HEADS, V_DIM), F32),
        compiler_params=_params(("parallel", "arbitrary"), 32),
        name="attention_decode",
    )(page_table.reshape(-1), lam, heads3(q), heads3(k_new), heads3(v_new), cache_k, cache_v, subln_g)
    return out.reshape(nb, ATT_WIDTH)


def _hgrn_gates(fx, log_lb, log1m_lb, one_m_lb):
    e = jnp.exp(-jnp.abs(fx))
    log_sig = jnp.minimum(fx, 0.0) - jnp.log(1.0 + e)
    b = log1m_lb + log_sig
    log_f = jnp.maximum(log_lb, b) + jnp.log(1.0 + jnp.exp(-jnp.abs(log_lb - b)))
    k = one_m_lb * (jnp.where(fx >= 0.0, e, 1.0) / (1.0 + e))
    return log_f, k


def _rnn_out(o, gate, norm_g):
    r = lax.rsqrt(jnp.mean(o * o, axis=-1, keepdims=True) + EPS)
    return o * r * norm_g * (gate * _sigmoid(gate))


def _hgrn_prompt_kernel(q_ref, f_ref, i_ref, g_ref, lb_ref, ng_ref, tri_ref, o_ref, s_ref, s_scr):
    c = pl.program_id(1)
    n_chunks = pl.num_programs(1)
    ch, sub = HGRN_CHUNK, HGRN_SUB
    n_sub = ch // sub

    @pl.when(c == 0)
    def _():
        s_scr[...] = jnp.zeros_like(s_scr)

    tri = tri_ref[...]
    row_c = lax.broadcasted_iota(jnp.int32, (ch, RNN_DIM), 0)
    row_s = lax.broadcasted_iota(jnp.int32, (sub, RNN_DIM), 0)
    lane_s = lax.broadcasted_iota(jnp.int32, (sub, ch), 1)
    eye = (lax.broadcasted_iota(jnp.int32, (RNN_DIM, RNN_DIM), 0)
           == lax.broadcasted_iota(jnp.int32, (RNN_DIM, RNN_DIM), 1))

    for h in range(N_RNN_HEADS):
        lo, hi = h * RNN_DIM, (h + 1) * RNN_DIM
        log_f, k = _hgrn_gates(f_ref[:, lo:hi], lb_ref[0:1, lo:hi], lb_ref[1:2, lo:hi], lb_ref[2:3, lo:hi])
        qx = q_ref[:, lo:hi]
        q = qx * _sigmoid(qx)
        v_bf = i_ref[:, lo:hi].astype(BF16)
        b = sum(jnp.dot(tri, part, preferred_element_type=F32) for part in _split_bf16(log_f, 3))
        state = s_scr[h]
        o = jnp.dot((q * jnp.exp(b)).astype(BF16), state.astype(BF16), preferred_element_type=F32)

        a_rows = []
        for blk in range(n_sub):
            r0 = blk * sub
            q_b, b_b, k_b = q[r0:r0 + sub], b[r0:r0 + sub], k[r0:r0 + sub]
            if blk == 0:
                a_blk = jnp.zeros((sub, ch), F32)
            else:
                ref_row = b[r0 - 1:r0]
                q_dec = (q_b * jnp.exp(b_b - ref_row)).astype(BF16)
                k_dec = jnp.where(row_c < r0, k * jnp.exp(jnp.minimum(ref_row - b, 0.0)), 0.0).astype(BF16)
                a_blk = lax.dot_general(q_dec, k_dec, NT_DIMS, preferred_element_type=F32)
            for s in range(sub):
                w = q_b * jnp.exp(jnp.minimum(b_b - b_b[s:s + 1], 0.0)) * k_b[s:s + 1]
                col = jnp.sum(jnp.where(row_s >= s, w, 0.0), axis=-1, keepdims=True)
                a_blk = a_blk + jnp.where(lane_s == r0 + s, col, 0.0)
            a_rows.append(a_blk)
        a = jnp.concatenate(a_rows, axis=0)
        o = o + jnp.dot(a.astype(BF16), v_bf, preferred_element_type=F32)

        b_last = b[ch - 1:ch]
        decay_col = jnp.sum(jnp.where(eye, jnp.exp(b_last), 0.0), axis=1, keepdims=True)
        k_end = (k * jnp.exp(b_last - b)).T.astype(BF16)
        new_state = state * decay_col + jnp.dot(k_end, v_bf, preferred_element_type=F32)
        s_scr[h] = new_state
        o_ref[:, lo:hi] = _rnn_out(o, g_ref[:, lo:hi], ng_ref[...])

        @pl.when(c == n_chunks - 1)
        def _():
            s_ref[0, h] = new_state


def _hgrn_prompt(r, lb_rows, norm_g, tri, batch, seq):
    ch = HGRN_CHUNK
    nc = seq // ch
    col = lambda j: pl.BlockSpec((ch, RNN_WIDTH), lambda b, c: (b * nc + c, j))
    fixed = lambda b, c: (0, 0)
    return pl.pallas_call(
        _hgrn_prompt_kernel,
        grid=(batch, nc),
        in_specs=[col(0), col(1), col(2), col(3),
                  pl.BlockSpec((3, RNN_WIDTH), fixed),
                  pl.BlockSpec((1, RNN_DIM), fixed),
                  pl.BlockSpec((ch, ch), fixed)],
        out_specs=[
            pl.BlockSpec((ch, RNN_WIDTH), lambda b, c: (b * nc + c, 0)),
            pl.BlockSpec((1, N_RNN_HEADS, RNN_DIM, RNN_DIM), lambda b, c: (b, 0, 0, 0)),
        ],
        out_shape=[
            jax.ShapeDtypeStruct((batch * seq, RNN_WIDTH), F32),
            jax.ShapeDtypeStruct((batch, N_RNN_HEADS, RNN_DIM, RNN_DIM), F32),
        ],
        scratch_shapes=[pltpu.VMEM((N_RNN_HEADS, RNN_DIM, RNN_DIM), F32)],
        compiler_params=_params(("parallel", "arbitrary"), 32),
        name="hgrn_prompt",
    )(r, r, r, r, lb_rows, norm_g, tri)


def _hgrn_decode_kernel(r_ref, s_ref, lb_ref, ng_ref, o_ref, so_ref):
    eye = (lax.broadcasted_iota(jnp.int32, (RNN_DIM, RNN_DIM), 0)
           == lax.broadcasted_iota(jnp.int32, (RNN_DIM, RNN_DIM), 1))
    to_col = lambda row: jnp.sum(jnp.where(eye, row, 0.0), axis=1, keepdims=True)
    for h in range(N_RNN_HEADS):
        seg = lambda j: r_ref[0, :, j * RNN_WIDTH + h * RNN_DIM:j * RNN_WIDTH + (h + 1) * RNN_DIM]
        lo, hi = h * RNN_DIM, (h + 1) * RNN_DIM
        log_f, k = _hgrn_gates(seg(1), lb_ref[0:1, lo:hi], lb_ref[1:2, lo:hi], lb_ref[2:3, lo:hi])
        qx = seg(0)
        q = qx * _sigmoid(qx)
        new_state = s_ref[0, h] * to_col(jnp.exp(log_f)) + to_col(k) * seg(2)
        so_ref[0, h] = new_state
        o = jnp.sum(to_col(q) * new_state, axis=0, keepdims=True)
        o_ref[0, :, lo:hi] = _rnn_out(o, seg(3), ng_ref[...])


def _hgrn_decode(r, state, layer, lb_rows, norm_g):
    nb = r.shape[0]
    out, new_state = pl.pallas_call(
        _hgrn_decode_kernel,
        grid=(nb,),
        in_specs=[
            pl.BlockSpec((1, 1, 4 * RNN_WIDTH), lambda b: (b, 0, 0)),
            pl.BlockSpec((None, 1, N_RNN_HEADS, RNN_DIM, RNN_DIM), lambda b: (layer, b, 0, 0, 0)),
            pl.BlockSpec((3, RNN_WIDTH), lambda b: (0, 0)),
            pl.BlockSpec((1, RNN_DIM), lambda b: (0, 0)),
        ],
        out_specs=[
            pl.BlockSpec((1, 1, RNN_WIDTH), lambda b: (b, 0, 0)),
            pl.BlockSpec((1, N_RNN_HEADS, RNN_DIM, RNN_DIM), lambda b: (b, 0, 0, 0)),
        ],
        out_shape=[
            jax.ShapeDtypeStruct((nb, 1, RNN_WIDTH), F32),
            jax.ShapeDtypeStruct((nb, N_RNN_HEADS, RNN_DIM, RNN_DIM), F32),
        ],
        compiler_params=_params(("parallel",), 32),
        name="hgrn_decode",
    )(r.reshape(nb, 1, 4 * RNN_WIDTH), state, lb_rows, norm_g)
    return out.reshape(nb, RNN_WIDTH), new_state


def _outproj_kernel(x_ref, att_ref, rnn_ref, w_ref, g_ref, *rest, routed, exact):
    if routed:
        wr_ref, xo_ref, h_ref, comb_ref = rest
    else:
        xo_ref, h_ref = rest
    mix = (_mm(att_ref[...], w_ref[:ATT_WIDTH, :], exact)
           + _mm(rnn_ref[...], w_ref[ATT_WIDTH:, :], exact))
    x = x_ref[...] + mix
    xo_ref[...] = x
    r = lax.rsqrt(jnp.mean(x * x, axis=-1, keepdims=True) + EPS)
    hf = x * r * g_ref[...]
    h_ref[...] = hf.astype(h_ref.dtype)
    if routed:
        logits = _mm(hf, wr_ref[...], True)
        lane = lax.broadcasted_iota(jnp.int32, logits.shape, 1)
        lane_f = lane.astype(F32)
        logits = jnp.where(lane < N_EXPERTS, logits, -jnp.inf)
        v1 = jnp.max(logits, axis=-1, keepdims=True)
        i1 = jnp.min(jnp.where(logits == v1, lane_f, float(LANES)), axis=-1, keepdims=True)
        first = lane_f == i1
        rest_logits = jnp.where(first, -jnp.inf, logits)
        v2 = jnp.max(rest_logits, axis=-1, keepdims=True)
        i2 = jnp.min(jnp.where(rest_logits == v2, lane_f, float(LANES)), axis=-1, keepdims=True)
        second = lane_f == i2
        e = jnp.exp(v2 - v1)
        g1 = 1.0 / (1.0 + e)
        comb_ref[...] = jnp.where(first, g1, 0.0) + jnp.where(second, e * g1, 0.0)


def _output_projection(x, att, rnn, w, ln_g, router, tm):
    exact = w.dtype == F32
    n = x.shape[0]
    row = lambda i: (i, 0)
    fixed = lambda i: (0, 0)
    in_specs = [
        pl.BlockSpec((tm, D_MODEL), row),
        pl.BlockSpec((tm, ATT_WIDTH), row),
        pl.BlockSpec((tm, RNN_WIDTH), row),
        pl.BlockSpec((D_MODEL, D_MODEL), fixed),
        pl.BlockSpec((1, D_MODEL), fixed),
    ]
    out_specs = [pl.BlockSpec((tm, D_MODEL), row), pl.BlockSpec((tm, D_MODEL), row)]
    out_shape = [jax.ShapeDtypeStruct((n, D_MODEL), F32), jax.ShapeDtypeStruct((n, D_MODEL), F32 if exact else BF16)]
    args = [x, att, rnn, w, ln_g]
    if router is not None:
        in_specs.append(pl.BlockSpec((D_MODEL, LANES), fixed))
        out_specs.append(pl.BlockSpec((tm, LANES), row))
        out_shape.append(jax.ShapeDtypeStruct((n, LANES), F32))
        args.append(router)
    return pl.pallas_call(
        functools.partial(_outproj_kernel, routed=router is not None, exact=exact),
        grid=(n // tm,),
        in_specs=in_specs,
        out_specs=out_specs,
        out_shape=out_shape,
        compiler_params=_params(("parallel",), 32),
        name="output_projection",
    )(*args)


def _ffn_kernel(x_ref, h_ref, wg_ref, wu_ref, wd_ref, y_ref, acc, *, exact):
    f = pl.program_id(1)

    @pl.when(f == 0)
    def _():
        acc[...] = jnp.zeros_like(acc)

    h = h_ref[...]
    g = _mm(h, wg_ref[...], exact)
    u = _mm(h, wu_ref[...], exact)
    acc[...] += _mm(g * _sigmoid(g) * u, wd_ref[...], exact)

    @pl.when(f == pl.num_programs(1) - 1)
    def _():
        y_ref[...] = x_ref[...] + acc[...]


def _dense_ffn(x, h, w_gu, w_dn, tm, tf):
    n = x.shape[0]
    d_ff = w_dn.shape[0]
    nf = d_ff // tf
    return pl.pallas_call(
        functools.partial(_ffn_kernel, exact=w_gu.dtype == F32),
        grid=(n // tm, nf),
        in_specs=[
            pl.BlockSpec((tm, D_MODEL), lambda i, f: (i, 0)),
            pl.BlockSpec((tm, D_MODEL), lambda i, f: (i, 0)),
            pl.BlockSpec((D_MODEL, tf), lambda i, f: (0, f)),
            pl.BlockSpec((D_MODEL, tf), lambda i, f: (0, f + nf)),
            pl.BlockSpec((tf, D_MODEL), lambda i, f: (f, 0)),
        ],
        out_specs=pl.BlockSpec((tm, D_MODEL), lambda i, f: (i, 0)),
        out_shape=jax.ShapeDtypeStruct((n, D_MODEL), F32),
        scratch_shapes=[pltpu.VMEM((tm, D_MODEL), F32)],
        compiler_params=_params(("parallel", "arbitrary"), 48),
        name="dense_ffn",
    )(x, h, w_gu, w_gu, w_dn)


def _moe_kernel(x_ref, h_ref, comb_ref, wg_ref, wu_ref, wd_ref, y_ref, acc):
    e = pl.program_id(1)
    f = pl.program_id(2)

    @pl.when((e == 0) & (f == 0))
    def _():
        acc[...] = jnp.zeros_like(acc)

    comb = comb_ref[...]
    lane = lax.broadcasted_iota(jnp.int32, comb.shape, 1)
    gate = jnp.sum(jnp.where(lane == e, comb, 0.0), axis=-1, keepdims=True)
    h = h_ref[...]
    g = jnp.dot(h, wg_ref[...], preferred_element_type=F32)
    u = jnp.dot(h, wu_ref[...], preferred_element_type=F32)
    a = (g * _sigmoid(g) * u).astype(BF16)
    acc[...] += gate * jnp.dot(a, wd_ref[...], preferred_element_type=F32)

    @pl.when((e == pl.num_programs(1) - 1) & (f == pl.num_programs(2) - 1))
    def _():
        y_ref[...] = x_ref[...] + acc[...]


def _moe_ffn(x, h, comb, w_gu, w_dn, tm, tf):
    n = x.shape[0]
    d_ff = w_dn.shape[1]
    nf = d_ff // tf
    return pl.pallas_call(
        _moe_kernel,
        grid=(n // tm, N_EXPERTS, nf),
        in_specs=[
            pl.BlockSpec((tm, D_MODEL), lambda i, e, f: (i, 0)),
            pl.BlockSpec((tm, D_MODEL), lambda i, e, f: (i, 0)),
            pl.BlockSpec((tm, LANES), lambda i, e, f: (i, 0)),
            pl.BlockSpec((None, D_MODEL, tf), lambda i, e, f: (e, 0, f)),
            pl.BlockSpec((None, D_MODEL, tf), lambda i, e, f: (e, 0, f + nf)),
            pl.BlockSpec((None, tf, D_MODEL), lambda i, e, f: (e, f, 0)),
        ],
        out_specs=pl.BlockSpec((tm, D_MODEL), lambda i, e, f: (i, 0)),
        out_shape=jax.ShapeDtypeStruct((n, D_MODEL), F32),
        scratch_shapes=[pltpu.VMEM((tm, D_MODEL), F32)],
        compiler_params=_params(("parallel", "arbitrary", "arbitrary"), 48),
        name="moe_ffn",
    )(x, h, comb, w_gu, w_gu, w_dn)


def _row(a):
    return a.reshape(1, -1).astype(F32)


def kernel(x_prompt, x_sample, cache_k, cache_v, state_hgrn, page_table, ln1_g, w_in, q_norm_g, k_norm_g,
           lam_p, subln_g, hgrn_lb, hgrn_norm_g, w_out, ln2_g, w_ffn_gu, w_ffn_dn, w_router, w_exp_gu,
           w_exp_dn):
    bp, tp, _ = x_prompt.shape
    bs = x_sample.shape[0]
    depth = w_in.shape[0]
    n_pool, page_size = cache_k.shape[1], cache_k.shape[2]
    past = page_table.shape[1] * page_size

    xp = x_prompt.reshape(bp * tp, D_MODEL)
    xs = x_sample.reshape(bs, D_MODEL)
    ck = cache_k.reshape(depth, n_pool, page_size * N_ATT_HEADS, V_DIM)
    cv = cache_v.reshape(depth, n_pool, page_size * N_ATT_HEADS, V_DIM)

    tabs_p = _rope_tables(0, tp)
    tabs_s = [jnp.broadcast_to(tab, (bs, LANES)) for tab in _rope_tables(past, 1)]
    lane = jnp.arange(LANES)
    bd = jnp.where((lane[:, None] // QK_DIM) == (lane[None, :] // QK_DIM), 1.0 / QK_DIM, 0.0).astype(BF16)
    tok = jnp.arange(HGRN_CHUNK)
    tri = (tok[None, :] <= tok[:, None]).astype(BF16)

    lbs = jnp.cumsum(jax.nn.softmax(hgrn_lb.astype(F32), axis=0), axis=0)
    lbs = lbs - lbs[0]

    n_p = bp * tp
    tm_p = min(256, n_p)
    tq = min(512, tp)
    tm_ffn = min(1024, n_p)
    outs = {name: [] for name in ("kp", "vp", "sp", "ks", "vs", "ss")}
    for l in range(depth):
        lam_init = 0.8 - 0.6 * math.exp(-0.3 * l)
        lp = lam_p[l].astype(F32)
        lam = (jnp.exp(jnp.sum(lp[0] * lp[1])) - jnp.exp(jnp.sum(lp[2] * lp[3])) + lam_init).reshape(1)
        w_in_bf = w_in[l].astype(BF16)
        w_out_bf = w_out[l].astype(BF16)
        qg, kg = _row(q_norm_g[l]), _row(k_norm_g[l])
        lb_rows = jnp.stack([jnp.log(lbs[l]), jnp.log1p(-lbs[l]), 1.0 - lbs[l]])
        sub_g, rnn_g = _row(subln_g[l]), _row(hgrn_norm_g[l])
        ln1, ln2 = _row(ln1_g[l]), _row(ln2_g[l])
        router = None
        if l % 2 == 1:
            router = jnp.pad(w_router[l // 2].astype(F32), ((0, 0), (0, LANES - N_EXPERTS)))

        q, k, v, r = _input_projection(xp, ln1, w_in_bf, qg, kg, bd, tabs_p, tm_p)
        att = _attention_prompt(q, k, v, lam, sub_g, lam_init, bp, tp, tq)
        rnn, s_fin = _hgrn_prompt(r, lb_rows, rnn_g, tri, bp, tp)
        proj_p = _output_projection(xp, att, rnn, w_out_bf, ln2, router, tm_p)
        outs["kp"].append(k.reshape(bp, tp, N_ATT_HEADS, V_DIM))
        outs["vp"].append(v.reshape(bp, tp, N_ATT_HEADS, V_DIM))
        outs["sp"].append(s_fin)

        q, k, v, r = _input_projection(xs, ln1, w_in[l], qg, kg, bd, tabs_s, bs)
        att = _attention_decode(q, k, v, ck, cv, page_table, l, lam, sub_g, lam_init)
        rnn, s_new = _hgrn_decode(r, state_hgrn, l, lb_rows, rnn_g)
        proj_s = _output_projection(xs, att, rnn, w_out[l], ln2, router, bs)
        outs["ks"].append(k.reshape(bs, 1, N_ATT_HEADS, V_DIM))
        outs["vs"].append(v.reshape(bs, 1, N_ATT_HEADS, V_DIM))
        outs["ss"].append(s_new)

        if l % 2 == 0:
            w_gu = w_ffn_gu[l // 2].astype(BF16)
            w_dn = w_ffn_dn[l // 2].astype(BF16)
            xp = _dense_ffn(proj_p[0], proj_p[1], w_gu, w_dn, tm_ffn, 1408)
            xs = _dense_ffn(proj_s[0], proj_s[1], w_ffn_gu[l // 2], w_ffn_dn[l // 2], bs, 256)
        else:
            w_gu = w_exp_gu[l // 2].astype(BF16)
            w_dn = w_exp_dn[l // 2].astype(BF16)
            xp = _moe_ffn(proj_p[0], proj_p[1], proj_p[2], w_gu, w_dn, tm_ffn, 896)
            xs = _moe_ffn(proj_s[0], proj_s[1].astype(BF16), proj_s[2], w_gu, w_dn, bs, 896)

    return (xp.reshape(bp, tp, D_MODEL), xs.reshape(bs, 1, D_MODEL),
            jnp.stack(outs["kp"]), jnp.stack(outs["vp"]), jnp.stack(outs["sp"]),
            jnp.stack(outs["ks"]), jnp.stack(outs["vs"]), jnp.stack(outs["ss"]))
```

```python
import functools
import math

import jax
import jax.numpy as jnp
import numpy as np
from jax import lax
from jax.experimental import pallas as pl
from jax.experimental.pallas import tpu as pltpu

F32 = jnp.float32
BF16 = jnp.bfloat16

D_MODEL = 1024
N_ATT_HEADS = 4
V_DIM = 128
QK_DIM = 64
ROT_DIM = 16
ROPE_THETA = 500000.0
N_RNN_HEADS = 4
RNN_DIM = 128
ATT_WIDTH = N_ATT_HEADS * V_DIM
RNN_WIDTH = N_RNN_HEADS * RNN_DIM
IN_WIDTH = 3 * ATT_WIDTH + 4 * RNN_WIDTH
N_EXPERTS = 8
EPS = 1e-6
NEG = -1e30
LANES = 128
V7X_VMEM_BYTES = 64 * 1024 * 1024
HGRN_CHUNK = 128
HGRN_SUB = 16

NT_DIMS = (((1,), (1,)), ((), ()))


def _params(semantics, vmem_mb):
    return pltpu.CompilerParams(dimension_semantics=semantics,
                                vmem_limit_bytes=vmem_mb * 1024 * 1024)


def _sigmoid(x):
    return 1.0 / (1.0 + jnp.exp(-x))


def _mm(a, b, exact, dims=None):
    if exact:
        a, b, prec = a.astype(F32), b.astype(F32), lax.Precision.HIGHEST
    else:
        a, b, prec = a.astype(BF16), b.astype(BF16), None
    if dims is None:
        return jnp.dot(a, b, precision=prec, preferred_element_type=F32)
    return lax.dot_general(a, b, dims, precision=prec, preferred_element_type=F32)


def _split_bf16(x, parts):
    out = []
    for _ in range(parts - 1):
        h = x.astype(BF16)
        out.append(h)
        x = x - h.astype(F32)
    out.append(x.astype(BF16))
    return out


def _group_mean_sq(x, bd):
    parts = _split_bf16(x * x, 2)
    return (jnp.dot(parts[0], bd, preferred_element_type=F32)
            + jnp.dot(parts[1], bd, preferred_element_type=F32))


def _qk_norm_rope(z, gain, bd, cos, sin_hi, sin_lo):
    r = lax.rsqrt(_group_mean_sq(z, bd) + EPS)
    y = z * r * gain
    half = ROT_DIM // 2
    up = pltpu.roll(y, LANES - half, axis=1)
    dn = pltpu.roll(y, half, axis=1)
    return y * cos + up * sin_hi + dn * sin_lo


def _inproj_kernel(x_ref, g_ref, w_ref, qg_ref, kg_ref, bd_ref, cos_ref, shi_ref, slo_ref,
                   q_ref, k_ref, v_ref, r_ref, kb_ref, vb_ref, *, exact):
    x = x_ref[...]
    r = lax.rsqrt(jnp.mean(x * x, axis=-1, keepdims=True) + EPS)
    hn = x * r * g_ref[...]
    if not exact:
        hn = hn.astype(BF16)
    bd = bd_ref[...]
    cos, shi, slo = cos_ref[...], shi_ref[...], slo_ref[...]
    for h in range(N_ATT_HEADS):
        lo, hi = h * LANES, (h + 1) * LANES
        zq = _mm(hn, w_ref[:, lo:hi], exact)
        q_ref[:, lo:hi] = _qk_norm_rope(zq, qg_ref[...], bd, cos, shi, slo)
        zk = _mm(hn, w_ref[:, ATT_WIDTH + lo:ATT_WIDTH + hi], exact)
        k_head = _qk_norm_rope(zk, kg_ref[...], bd, cos, shi, slo)
        k_ref[:, lo:hi] = k_head
        kb_ref[:, lo:hi] = k_head.astype(BF16)
    v = _mm(hn, w_ref[:, 2 * ATT_WIDTH:3 * ATT_WIDTH], exact)
    v_ref[...] = v
    vb_ref[...] = v.astype(BF16)
    for j in range(4):
        lo, hi = j * RNN_WIDTH, (j + 1) * RNN_WIDTH
        r_ref[:, lo:hi] = _mm(hn, w_ref[:, 3 * ATT_WIDTH + lo:3 * ATT_WIDTH + hi], exact)


def _input_projection(x, ln_g, w, qg, kg, bd, tabs, tm):
    n = x.shape[0]
    cos, shi, slo = tabs
    nt = cos.shape[0] // tm
    row = lambda i: (i, 0)
    fixed = lambda i: (0, 0)
    tab = lambda i: (i % nt, 0)
    return pl.pallas_call(
        functools.partial(_inproj_kernel, exact=w.dtype == F32),
        grid=(n // tm,),
        in_specs=[
            pl.BlockSpec((tm, D_MODEL), row),
            pl.BlockSpec((1, D_MODEL), fixed),
            pl.BlockSpec((D_MODEL, IN_WIDTH), fixed),
            pl.BlockSpec((1, LANES), fixed),
            pl.BlockSpec((1, LANES), fixed),
            pl.BlockSpec((LANES, LANES), fixed),
            pl.BlockSpec((tm, LANES), tab),
            pl.BlockSpec((tm, LANES), tab),
            pl.BlockSpec((tm, LANES), tab),
        ],
        out_specs=[
            pl.BlockSpec((tm, ATT_WIDTH), row),
            pl.BlockSpec((tm, ATT_WIDTH), row),
            pl.BlockSpec((tm, ATT_WIDTH), row),
            pl.BlockSpec((tm, 4 * RNN_WIDTH), row),
            pl.BlockSpec((tm, ATT_WIDTH), row),
            pl.BlockSpec((tm, ATT_WIDTH), row),
        ],
        out_shape=[
            jax.ShapeDtypeStruct((n, ATT_WIDTH), F32),
            jax.ShapeDtypeStruct((n, ATT_WIDTH), F32),
            jax.ShapeDtypeStruct((n, ATT_WIDTH), F32),
            jax.ShapeDtypeStruct((n, 4 * RNN_WIDTH), F32),
            jax.ShapeDtypeStruct((n, ATT_WIDTH), BF16),
            jax.ShapeDtypeStruct((n, ATT_WIDTH), BF16),
        ],
        compiler_params=_params(("parallel",), 48),
        name="input_projection",
    )(x, ln_g, w, qg, kg, bd, cos, shi, slo)


def _rope_tables(start, t):
    half = ROT_DIM // 2
    inv = np.power(ROPE_THETA, -np.arange(half, dtype=np.float64) * 2.0 / ROT_DIM)
    pos = start + np.arange(t)
    coarse = np.arange(pos[0] // LANES, pos[-1] // LANES + 1, dtype=np.float64)[:, None] * LANES * inv
    fine = np.arange(LANES, dtype=np.float64)[:, None] * inv
    const = lambda a: jnp.asarray(a.astype(np.float32))
    a_idx = jnp.asarray(pos // LANES - pos[0] // LANES)
    b_idx = jnp.asarray(pos % LANES)
    ca, sa = const(np.cos(coarse))[a_idx], const(np.sin(coarse))[a_idx]
    cb, sb = const(np.cos(fine))[b_idx], const(np.sin(fine))[b_idx]
    cos, sin = ca * cb - sa * sb, sa * cb + ca * sb
    one = jnp.ones((t, QK_DIM - ROT_DIM), F32)
    zero = jnp.zeros((t, QK_DIM - ROT_DIM), F32)
    zh = jnp.zeros((t, half), F32)
    comp = lambda a, b, rest: jnp.concatenate([a, b, rest], axis=1)
    c64 = comp(cos, cos, one)
    hi64 = comp(-sin, zh, zero)
    lo64 = comp(zh, sin, zero)
    two = lambda a: jnp.concatenate([a, a], axis=1)
    return two(c64), two(hi64), two(lo64)


def _attn_prompt_kernel(lam_ref, q_ref, k_ref, v_ref, g_ref, o_ref, acc, *, tq, tk, lam_scale):
    qi = pl.program_id(2)
    q = q_ref[...] * (QK_DIM ** -0.5)
    lane = lax.broadcasted_iota(jnp.int32, q.shape, 1)
    q_cat = jnp.concatenate([jnp.where(lane < QK_DIM, q, 0.0), jnp.where(lane >= QK_DIM, q, 0.0)],
                            axis=0).astype(BF16)
    acc[...] = jnp.zeros_like(acc)

    def tile(j, carry, diag):
        m, l = carry
        start = pl.multiple_of(j * tk, tk)
        s = lax.dot_general(q_cat, k_ref[pl.ds(start, tk), :], NT_DIMS, preferred_element_type=F32)
        if diag is not None:
            rows = lax.broadcasted_iota(jnp.int32, (2 * tq, tk), 0) % tq
            cols = lax.broadcasted_iota(jnp.int32, (2 * tq, tk), 1) + diag * tk
            s = jnp.where(cols <= rows, s, NEG)
        m_new = jnp.maximum(m, jnp.max(s, axis=-1, keepdims=True))
        alpha = jnp.exp(m - m_new)
        p = jnp.exp(s - m_new)
        l_new = alpha * l + jnp.sum(p, axis=-1, keepdims=True)
        acc[...] = alpha * acc[...] + jnp.dot(p.astype(BF16), v_ref[pl.ds(start, tk), :],
                                              preferred_element_type=F32)
        return m_new, l_new

    per_q = tq // tk
    carry = (jnp.full((2 * tq, 1), NEG, F32), jnp.zeros((2 * tq, 1), F32))
    carry = lax.fori_loop(0, qi * per_q, lambda j, c: tile(j, c, None), carry)
    for d in range(per_q):
        carry = tile(qi * per_q + d, carry, d)
    o2 = acc[...] / carry[1]
    o = o2[:tq] - lam_ref[0] * o2[tq:]
    r = lax.rsqrt(jnp.mean(o * o, axis=-1, keepdims=True) + EPS)
    o_ref[...] = o * r * g_ref[...] * lam_scale


def _attention_prompt(q, k, v, lam, subln_g, lam_init, batch, seq, tq, tk):
    nq = seq // tq
    kernel = functools.partial(_attn_prompt_kernel, tq=tq, tk=tk, lam_scale=1.0 - lam_init)
    return pl.pallas_call(
        kernel,
        grid=(batch, N_ATT_HEADS, nq),
        in_specs=[
            pl.BlockSpec(memory_space=pltpu.SMEM),
            pl.BlockSpec((tq, V_DIM), lambda b, h, i: (b * nq + i, h)),
            pl.BlockSpec((seq, V_DIM), lambda b, h, i: (b, h)),
            pl.BlockSpec((seq, V_DIM), lambda b, h, i: (b, h)),
            pl.BlockSpec((1, V_DIM), lambda b, h, i: (0, 0)),
        ],
        out_specs=pl.BlockSpec((tq, V_DIM), lambda b, h, i: (b * nq + i, h)),
        out_shape=jax.ShapeDtypeStruct((batch * seq, ATT_WIDTH), F32),
        scratch_shapes=[pltpu.VMEM((2 * tq, V_DIM), F32)],
        compiler_params=_params(("parallel", "parallel", "arbitrary"), 32),
        name="attention_prompt",
    )(lam, q, k, v, subln_g)


def _two_term(x):
    return jnp.concatenate(_split_bf16(x, 2), axis=0)


def _attn_decode_kernel(pt_ref, lam_ref, q_ref, kn_ref, vn_ref, *rest, n_pages, lam_scale):
    k_refs, v_refs = rest[:n_pages], rest[n_pages:2 * n_pages]
    g_ref, o_ref = rest[2 * n_pages:]
    rows_per_page = k_refs[0].shape[0]
    n_rows = 2 * N_ATT_HEADS
    lane = lax.broadcasted_iota(jnp.int32, (N_ATT_HEADS, V_DIM), 1)
    q4 = q_ref[0] * (QK_DIM ** -0.5)
    qm = jnp.concatenate([jnp.where(lane < QK_DIM, q4, 0.0), jnp.where(lane >= QK_DIM, q4, 0.0)], axis=0)
    q_cat = _two_term(qm)

    def two_term_dot(lhs_cat, rhs, dims):
        out = sum(lax.dot_general(lhs_cat, part, dims, preferred_element_type=F32)
                  for part in _split_bf16(rhs, 2))
        return out[:n_rows] + out[n_rows:]

    r_i = lax.broadcasted_iota(jnp.int32, (n_rows, rows_per_page), 0)
    c_i = lax.broadcasted_iota(jnp.int32, (n_rows, rows_per_page), 1)
    keep = (c_i % N_ATT_HEADS) == (r_i % N_ATT_HEADS)
    scores = [jnp.where(keep, two_term_dot(q_cat, k_ref[...], NT_DIMS), NEG) for k_ref in k_refs]

    prod = q4 * kn_ref[0]
    s_new = jnp.concatenate([jnp.sum(jnp.where(lane < QK_DIM, prod, 0.0), axis=-1, keepdims=True),
                             jnp.sum(jnp.where(lane >= QK_DIM, prod, 0.0), axis=-1, keepdims=True)], axis=0)
    m = s_new
    for s in scores:
        m = jnp.maximum(m, jnp.max(s, axis=-1, keepdims=True))
    p_new = jnp.exp(s_new - m)
    vn = vn_ref[0]
    l = p_new
    acc = p_new * jnp.concatenate([vn, vn], axis=0)
    for s, v_ref in zip(scores, v_refs):
        p = jnp.exp(s - m)
        l = l + jnp.sum(p, axis=-1, keepdims=True)
        acc = acc + two_term_dot(_two_term(p), v_ref[...], (((1,), (0,)), ((), ())))
    o8 = acc / l
    o = o8[:N_ATT_HEADS] - lam_ref[0] * o8[N_ATT_HEADS:]
    r = lax.rsqrt(jnp.mean(o * o, axis=-1, keepdims=True) + EPS)
    o_ref[0] = o * r * g_ref[...] * lam_scale


def _attention_decode(q, k_new, v_new, cache_k, cache_v, page_table, layer, lam, subln_g, lam_init):
    nb = q.shape[0]
    n_pages = page_table.shape[1]
    rows = cache_k.shape[2]
    heads3 = lambda a: a.reshape(nb, N_ATT_HEADS, V_DIM)
    tok = lambda b, pt: (b, 0, 0)
    page_specs = [pl.BlockSpec((None, None, rows, V_DIM),
                               functools.partial(lambda p, b, pt: (layer, pt[b * n_pages + p], 0, 0), p))
                  for p in range(n_pages)]
    kernel = functools.partial(_attn_decode_kernel, n_pages=n_pages, lam_scale=1.0 - lam_init)
    out = pl.pallas_call(
        kernel,
        grid_spec=pltpu.PrefetchScalarGridSpec(
            num_scalar_prefetch=1,
            grid=(nb,),
            in_specs=[
                pl.BlockSpec(memory_space=pltpu.SMEM),
                pl.BlockSpec((1, N_ATT_HEADS, V_DIM), tok),
                pl.BlockSpec((1, N_ATT_HEADS, V_DIM), tok),
                pl.BlockSpec((1, N_ATT_HEADS, V_DIM), tok),
                *page_specs, *page_specs,
                pl.BlockSpec((1, V_DIM), lambda b, pt: (0, 0)),
            ],
            out_specs=pl.BlockSpec((1, N_ATT_HEADS, V_DIM), tok),
        ),
        out_shape=jax.ShapeDtypeStruct((nb, N_ATT_HEADS, V_DIM), F32),
        compiler_params=_params(("parallel",), 40),
        name="attention_decode",
    )(page_table.reshape(-1), lam, heads3(q), heads3(k_new), heads3(v_new),
      *([cache_k] * n_pages), *([cache_v] * n_pages), subln_g)
    return out.reshape(nb, ATT_WIDTH)


def _hgrn_gates(fx, log_lb, log1m_lb, one_m_lb):
    e = jnp.exp(-jnp.abs(fx))
    log_sig = jnp.minimum(fx, 0.0) - jnp.log(1.0 + e)
    b = log1m_lb + log_sig
    log_f = jnp.maximum(log_lb, b) + jnp.log(1.0 + jnp.exp(-jnp.abs(log_lb - b)))
    k = one_m_lb * (jnp.where(fx >= 0.0, e, 1.0) / (1.0 + e))
    return log_f, k


def _rnn_out(o, gate, norm_g):
    r = lax.rsqrt(jnp.mean(o * o, axis=-1, keepdims=True) + EPS)
    return o * r * norm_g * (gate * _sigmoid(gate))


def _hgrn_prompt_kernel(q_ref, f_ref, i_ref, g_ref, lb_ref, ng_ref, tri_ref, o_ref, s_ref, s_scr):
    c = pl.program_id(1)
    n_chunks = pl.num_programs(1)
    ch, sub = HGRN_CHUNK, HGRN_SUB
    n_sub = ch // sub

    @pl.when(c == 0)
    def _():
        s_scr[...] = jnp.zeros_like(s_scr)

    tri = tri_ref[...]
    row_c = lax.broadcasted_iota(jnp.int32, (ch, RNN_DIM), 0)
    row_s = lax.broadcasted_iota(jnp.int32, (sub, RNN_DIM), 0)
    lane_s = lax.broadcasted_iota(jnp.int32, (sub, ch), 1)
    eye = (lax.broadcasted_iota(jnp.int32, (RNN_DIM, RNN_DIM), 0)
           == lax.broadcasted_iota(jnp.int32, (RNN_DIM, RNN_DIM), 1))

    for h in range(N_RNN_HEADS):
        lo, hi = h * RNN_DIM, (h + 1) * RNN_DIM
        log_f, k = _hgrn_gates(f_ref[:, lo:hi], lb_ref[0:1, lo:hi], lb_ref[1:2, lo:hi], lb_ref[2:3, lo:hi])
        qx = q_ref[:, lo:hi]
        q = qx * _sigmoid(qx)
        v_bf = i_ref[:, lo:hi].astype(BF16)
        b = sum(jnp.dot(tri, part, preferred_element_type=F32) for part in _split_bf16(log_f, 3))
        state = s_scr[h]
        o = jnp.dot((q * jnp.exp(b)).astype(BF16), state.astype(BF16), preferred_element_type=F32)

        a_rows = []
        for blk in range(n_sub):
            r0 = blk * sub
            q_b, b_b, k_b = q[r0:r0 + sub], b[r0:r0 + sub], k[r0:r0 + sub]
            if blk == 0:
                a_blk = jnp.zeros((sub, ch), F32)
            else:
                ref_row = b[r0 - 1:r0]
                q_dec = (q_b * jnp.exp(b_b - ref_row)).astype(BF16)
                k_dec = jnp.where(row_c < r0, k * jnp.exp(jnp.minimum(ref_row - b, 0.0)), 0.0).astype(BF16)
                a_blk = lax.dot_general(q_dec, k_dec, NT_DIMS, preferred_element_type=F32)
            for s in range(sub):
                w = q_b * jnp.exp(jnp.minimum(b_b - b_b[s:s + 1], 0.0)) * k_b[s:s + 1]
                col = jnp.sum(jnp.where(row_s >= s, w, 0.0), axis=-1, keepdims=True)
                a_blk = a_blk + jnp.where(lane_s == r0 + s, col, 0.0)
            a_rows.append(a_blk)
        a = jnp.concatenate(a_rows, axis=0)
        o = o + jnp.dot(a.astype(BF16), v_bf, preferred_element_type=F32)

        b_last = b[ch - 1:ch]
        decay_col = jnp.sum(jnp.where(eye, jnp.exp(b_last), 0.0), axis=1, keepdims=True)
        k_end = (k * jnp.exp(b_last - b)).T.astype(BF16)
        new_state = state * decay_col + jnp.dot(k_end, v_bf, preferred_element_type=F32)
        s_scr[h] = new_state
        o_ref[:, lo:hi] = _rnn_out(o, g_ref[:, lo:hi], ng_ref[...])

        @pl.when(c == n_chunks - 1)
        def _():
            s_ref[0, h] = new_state


def _hgrn_prompt(r, lb_rows, norm_g, tri, batch, seq):
    ch = HGRN_CHUNK
    nc = seq // ch
    col = lambda j: pl.BlockSpec((ch, RNN_WIDTH), lambda b, c: (b * nc + c, j))
    fixed = lambda b, c: (0, 0)
    return pl.pallas_call(
        _hgrn_prompt_kernel,
        grid=(batch, nc),
        in_specs=[col(0), col(1), col(2), col(3),
                  pl.BlockSpec((3, RNN_WIDTH), fixed),
                  pl.BlockSpec((1, RNN_DIM), fixed),
                  pl.BlockSpec((ch, ch), fixed)],
        out_specs=[
            pl.BlockSpec((ch, RNN_WIDTH), lambda b, c: (b * nc + c, 0)),
            pl.BlockSpec((1, N_RNN_HEADS, RNN_DIM, RNN_DIM), lambda b, c: (b, 0, 0, 0)),
        ],
        out_shape=[
            jax.ShapeDtypeStruct((batch * seq, RNN_WIDTH), F32),
            jax.ShapeDtypeStruct((batch, N_RNN_HEADS, RNN_DIM, RNN_DIM), F32),
        ],
        scratch_shapes=[pltpu.VMEM((N_RNN_HEADS, RNN_DIM, RNN_DIM), F32)],
        compiler_params=_params(("parallel", "arbitrary"), 32),
        name="hgrn_prompt",
    )(r, r, r, r, lb_rows, norm_g, tri)


def _hgrn_decode_kernel(r_ref, s_ref, lb_ref, ng_ref, o_ref, so_ref):
    eye = (lax.broadcasted_iota(jnp.int32, (RNN_DIM, RNN_DIM), 0)
           == lax.broadcasted_iota(jnp.int32, (RNN_DIM, RNN_DIM), 1))
    to_col = lambda row: jnp.sum(jnp.where(eye, row, 0.0), axis=1, keepdims=True)
    for h in range(N_RNN_HEADS):
        seg = lambda j: r_ref[0, :, j * RNN_WIDTH + h * RNN_DIM:j * RNN_WIDTH + (h + 1) * RNN_DIM]
        lo, hi = h * RNN_DIM, (h + 1) * RNN_DIM
        log_f, k = _hgrn_gates(seg(1), lb_ref[0:1, lo:hi], lb_ref[1:2, lo:hi], lb_ref[2:3, lo:hi])
        qx = seg(0)
        q = qx * _sigmoid(qx)
        new_state = s_ref[0, h] * to_col(jnp.exp(log_f)) + to_col(k) * seg(2)
        so_ref[0, h] = new_state
        o = jnp.sum(to_col(q) * new_state, axis=0, keepdims=True)
        o_ref[0, :, lo:hi] = _rnn_out(o, seg(3), ng_ref[...])


def _hgrn_decode(r, state, layer, lb_rows, norm_g):
    nb = r.shape[0]
    out, new_state = pl.pallas_call(
        _hgrn_decode_kernel,
        grid=(nb,),
        in_specs=[
            pl.BlockSpec((1, 1, 4 * RNN_WIDTH), lambda b: (b, 0, 0)),
            pl.BlockSpec((None, 1, N_RNN_HEADS, RNN_DIM, RNN_DIM), lambda b: (layer, b, 0, 0, 0)),
            pl.BlockSpec((3, RNN_WIDTH), lambda b: (0, 0)),
            pl.BlockSpec((1, RNN_DIM), lambda b: (0, 0)),
        ],
        out_specs=[
            pl.BlockSpec((1, 1, RNN_WIDTH), lambda b: (b, 0, 0)),
            pl.BlockSpec((1, N_RNN_HEADS, RNN_DIM, RNN_DIM), lambda b: (b, 0, 0, 0)),
        ],
        out_shape=[
            jax.ShapeDtypeStruct((nb, 1, RNN_WIDTH), F32),
            jax.ShapeDtypeStruct((nb, N_RNN_HEADS, RNN_DIM, RNN_DIM), F32),
        ],
        compiler_params=_params(("parallel",), 32),
        name="hgrn_decode",
    )(r.reshape(nb, 1, 4 * RNN_WIDTH), state, lb_rows, norm_g)
    return out.reshape(nb, RNN_WIDTH), new_state


def _outproj_kernel(x_ref, att_ref, rnn_ref, w_ref, g_ref, *rest, routed, exact):
    if routed:
        wr_ref, xo_ref, h_ref, comb_ref = rest
    else:
        xo_ref, h_ref = rest
    mix = (_mm(att_ref[...], w_ref[:ATT_WIDTH, :], exact)
           + _mm(rnn_ref[...], w_ref[ATT_WIDTH:, :], exact))
    x = x_ref[...] + mix
    xo_ref[...] = x
    r = lax.rsqrt(jnp.mean(x * x, axis=-1, keepdims=True) + EPS)
    hf = x * r * g_ref[...]
    h_ref[...] = hf.astype(h_ref.dtype)
    if routed:
        logits = _mm(hf, wr_ref[...], True)
        lane = lax.broadcasted_iota(jnp.int32, logits.shape, 1)
        lane_f = lane.astype(F32)
        logits = jnp.where(lane < N_EXPERTS, logits, -jnp.inf)
        v1 = jnp.max(logits, axis=-1, keepdims=True)
        i1 = jnp.min(jnp.where(logits == v1, lane_f, float(LANES)), axis=-1, keepdims=True)
        first = lane_f == i1
        rest_logits = jnp.where(first, -jnp.inf, logits)
        v2 = jnp.max(rest_logits, axis=-1, keepdims=True)
        i2 = jnp.min(jnp.where(rest_logits == v2, lane_f, float(LANES)), axis=-1, keepdims=True)
        second = lane_f == i2
        e = jnp.exp(v2 - v1)
        g1 = 1.0 / (1.0 + e)
        comb_ref[...] = jnp.where(first, g1, 0.0) + jnp.where(second, e * g1, 0.0)


def _output_projection(x, att, rnn, w, ln_g, router, tm):
    exact = w.dtype == F32
    n = x.shape[0]
    row = lambda i: (i, 0)
    fixed = lambda i: (0, 0)
    in_specs = [
        pl.BlockSpec((tm, D_MODEL), row),
        pl.BlockSpec((tm, ATT_WIDTH), row),
        pl.BlockSpec((tm, RNN_WIDTH), row),
        pl.BlockSpec((D_MODEL, D_MODEL), fixed),
        pl.BlockSpec((1, D_MODEL), fixed),
    ]
    out_specs = [pl.BlockSpec((tm, D_MODEL), row), pl.BlockSpec((tm, D_MODEL), row)]
    out_shape = [jax.ShapeDtypeStruct((n, D_MODEL), F32), jax.ShapeDtypeStruct((n, D_MODEL), F32 if exact else BF16)]
    args = [x, att, rnn, w, ln_g]
    if router is not None:
        in_specs.append(pl.BlockSpec((D_MODEL, LANES), fixed))
        out_specs.append(pl.BlockSpec((tm, LANES), row))
        out_shape.append(jax.ShapeDtypeStruct((n, LANES), F32))
        args.append(router)
    return pl.pallas_call(
        functools.partial(_outproj_kernel, routed=router is not None, exact=exact),
        grid=(n // tm,),
        in_specs=in_specs,
        out_specs=out_specs,
        out_shape=out_shape,
        compiler_params=_params(("parallel",), 32),
        name="output_projection",
    )(*args)


def _ffn_kernel(x_ref, h_ref, wg_ref, wu_ref, wd_ref, y_ref, acc, *, exact):
    f = pl.program_id(1)

    @pl.when(f == 0)
    def _():
        acc[...] = jnp.zeros_like(acc)

    h = h_ref[...]
    g = _mm(h, wg_ref[...], exact)
    u = _mm(h, wu_ref[...], exact)
    acc[...] += _mm(g * _sigmoid(g) * u, wd_ref[...], exact)

    @pl.when(f == pl.num_programs(1) - 1)
    def _():
        y_ref[...] = x_ref[...] + acc[...]


def _dense_ffn(x, h, w_gu, w_dn, tm, tf):
    n = x.shape[0]
    d_ff = w_dn.shape[0]
    nf = d_ff // tf
    return pl.pallas_call(
        functools.partial(_ffn_kernel, exact=w_gu.dtype == F32),
        grid=(n // tm, nf),
        in_specs=[
            pl.BlockSpec((tm, D_MODEL), lambda i, f: (i, 0)),
            pl.BlockSpec((tm, D_MODEL), lambda i, f: (i, 0)),
            pl.BlockSpec((D_MODEL, tf), lambda i, f: (0, f)),
            pl.BlockSpec((D_MODEL, tf), lambda i, f: (0, f + nf)),
            pl.BlockSpec((tf, D_MODEL), lambda i, f: (f, 0)),
        ],
        out_specs=pl.BlockSpec((tm, D_MODEL), lambda i, f: (i, 0)),
        out_shape=jax.ShapeDtypeStruct((n, D_MODEL), F32),
        scratch_shapes=[pltpu.VMEM((tm, D_MODEL), F32)],
        compiler_params=_params(("parallel", "arbitrary"), 48),
        name="dense_ffn",
    )(x, h, w_gu, w_gu, w_dn)


def _moe_kernel(x_ref, h_ref, comb_ref, wg_ref, wu_ref, wd_ref, y_ref, acc):
    e = pl.program_id(1)
    f = pl.program_id(2)

    @pl.when((e == 0) & (f == 0))
    def _():
        acc[...] = jnp.zeros_like(acc)

    comb = comb_ref[...]
    lane = lax.broadcasted_iota(jnp.int32, comb.shape, 1)
    gate = jnp.sum(jnp.where(lane == e, comb, 0.0), axis=-1, keepdims=True)
    h = h_ref[...]
    g = jnp.dot(h, wg_ref[...], preferred_element_type=F32)
    u = jnp.dot(h, wu_ref[...], preferred_element_type=F32)
    a = (g * _sigmoid(g) * u).astype(BF16)
    acc[...] += gate * jnp.dot(a, wd_ref[...], preferred_element_type=F32)

    @pl.when((e == pl.num_programs(1) - 1) & (f == pl.num_programs(2) - 1))
    def _():
        y_ref[...] = x_ref[...] + acc[...]


def _moe_ffn(x, h, comb, w_gu, w_dn, tm, tf):
    n = x.shape[0]
    d_ff = w_dn.shape[1]
    nf = d_ff // tf
    return pl.pallas_call(
        _moe_kernel,
        grid=(n // tm, N_EXPERTS, nf),
        in_specs=[
            pl.BlockSpec((tm, D_MODEL), lambda i, e, f: (i, 0)),
            pl.BlockSpec((tm, D_MODEL), lambda i, e, f: (i, 0)),
            pl.BlockSpec((tm, LANES), lambda i, e, f: (i, 0)),
            pl.BlockSpec((None, D_MODEL, tf), lambda i, e, f: (e, 0, f)),
            pl.BlockSpec((None, D_MODEL, tf), lambda i, e, f: (e, 0, f + nf)),
            pl.BlockSpec((None, tf, D_MODEL), lambda i, e, f: (e, f, 0)),
        ],
        out_specs=pl.BlockSpec((tm, D_MODEL), lambda i, e, f: (i, 0)),
        out_shape=jax.ShapeDtypeStruct((n, D_MODEL), F32),
        scratch_shapes=[pltpu.VMEM((tm, D_MODEL), F32)],
        compiler_params=_params(("parallel", "arbitrary", "arbitrary"), 48),
        name="moe_ffn",
    )(x, h, comb, w_gu, w_gu, w_dn)


def _row(a):
    return a.reshape(1, -1).astype(F32)


def kernel(x_prompt, x_sample, cache_k, cache_v, state_hgrn, page_table, ln1_g, w_in, q_norm_g, k_norm_g,
           lam_p, subln_g, hgrn_lb, hgrn_norm_g, w_out, ln2_g, w_ffn_gu, w_ffn_dn, w_router, w_exp_gu,
           w_exp_dn):
    bp, tp, _ = x_prompt.shape
    bs = x_sample.shape[0]
    depth = w_in.shape[0]
    n_pool, page_size = cache_k.shape[1], cache_k.shape[2]
    past = page_table.shape[1] * page_size

    xp = x_prompt.reshape(bp * tp, D_MODEL)
    xs = x_sample.reshape(bs, D_MODEL)
    ck = cache_k.reshape(depth, n_pool, page_size * N_ATT_HEADS, V_DIM)
    cv = cache_v.reshape(depth, n_pool, page_size * N_ATT_HEADS, V_DIM)

    tabs_p = _rope_tables(0, tp)
    tabs_s = [jnp.broadcast_to(tab, (bs, LANES)) for tab in _rope_tables(past, 1)]
    lane = jnp.arange(LANES)
    bd = jnp.where((lane[:, None] // QK_DIM) == (lane[None, :] // QK_DIM), 1.0 / QK_DIM, 0.0).astype(BF16)
    tok = jnp.arange(HGRN_CHUNK)
    tri = (tok[None, :] <= tok[:, None]).astype(BF16)

    lbs = jnp.cumsum(jax.nn.softmax(hgrn_lb.astype(F32), axis=0), axis=0)
    lbs = lbs - lbs[0]

    n_p = bp * tp
    tm_p = min(256, n_p)
    tq = min(512, tp)
    tk = min(512, tq)
    tm_ffn = min(1024, n_p)
    outs = {name: [] for name in ("kp", "vp", "sp", "ks", "vs", "ss")}
    for l in range(depth):
        lam_init = 0.8 - 0.6 * math.exp(-0.3 * l)
        lp = lam_p[l].astype(F32)
        lam = (jnp.exp(jnp.sum(lp[0] * lp[1])) - jnp.exp(jnp.sum(lp[2] * lp[3])) + lam_init).reshape(1)
        w_in_bf = w_in[l].astype(BF16)
        w_out_bf = w_out[l].astype(BF16)
        qg, kg = _row(q_norm_g[l]), _row(k_norm_g[l])
        lb_rows = jnp.stack([jnp.log(lbs[l]), jnp.log1p(-lbs[l]), 1.0 - lbs[l]])
        sub_g, rnn_g = _row(subln_g[l]), _row(hgrn_norm_g[l])
        ln1, ln2 = _row(ln1_g[l]), _row(ln2_g[l])
        router = None
        if l % 2 == 1:
            router = jnp.pad(w_router[l // 2].astype(F32), ((0, 0), (0, LANES - N_EXPERTS)))

        q, k, v, r, k_bf, v_bf = _input_projection(xp, ln1, w_in_bf, qg, kg, bd, tabs_p, tm_p)
        att = _attention_prompt(q, k_bf, v_bf, lam, sub_g, lam_init, bp, tp, tq, tk)
        rnn, s_fin = _hgrn_prompt(r, lb_rows, rnn_g, tri, bp, tp)
        proj_p = _output_projection(xp, att, rnn, w_out_bf, ln2, router, tm_p)
        outs["kp"].append(k.reshape(bp, tp, N_ATT_HEADS, V_DIM))
        outs["vp"].append(v.reshape(bp, tp, N_ATT_HEADS, V_DIM))
        outs["sp"].append(s_fin)

        q, k, v, r, _, _ = _input_projection(xs, ln1, w_in[l], qg, kg, bd, tabs_s, bs)
        att = _attention_decode(q, k, v, ck, cv, page_table, l, lam, sub_g, lam_init)
        rnn, s_new = _hgrn_decode(r, state_hgrn, l, lb_rows, rnn_g)
        proj_s = _output_projection(xs, att, rnn, w_out[l], ln2, router, bs)
        outs["ks"].append(k.reshape(bs, 1, N_ATT_HEADS, V_DIM))
        outs["vs"].append(v.reshape(bs, 1, N_ATT_HEADS, V_DIM))
        outs["ss"].append(s_new)

        if l % 2 == 0:
            w_gu = w_ffn_gu[l // 2].astype(BF16)
            w_dn = w_ffn_dn[l // 2].astype(BF16)
            xp = _dense_ffn(proj_p[0], proj_p[1], w_gu, w_dn, tm_ffn, 1408)
            xs = _dense_ffn(proj_s[0], proj_s[1], w_ffn_gu[l // 2], w_ffn_dn[l // 2], bs, 256)
        else:
            w_gu = w_exp_gu[l // 2].astype(BF16)
            w_dn = w_exp_dn[l // 2].astype(BF16)
            xp = _moe_ffn(proj_p[0], proj_p[1], proj_p[2], w_gu, w_dn, tm_ffn, 896)
            xs = _moe_ffn(proj_s[0], proj_s[1].astype(BF16), proj_s[2], w_gu, w_dn, bs, 896)

    return (xp.reshape(bp, tp, D_MODEL), xs.reshape(bs, 1, D_MODEL),
            jnp.stack(outs["kp"]), jnp.stack(outs["vp"]), jnp.stack(outs["sp"]),
            jnp.stack(outs["ks"]), jnp.stack(outs["vs"]), jnp.stack(outs["ss"]))
```

```python
import functools
import math

import jax
import jax.numpy as jnp
import numpy as np
from jax import lax
from jax.experimental import pallas as pl
from jax.experimental.pallas import tpu as pltpu

F32 = jnp.float32
BF16 = jnp.bfloat16

D_MODEL = 1024
N_ATT_HEADS = 4
V_DIM = 128
QK_DIM = 64
ROT_DIM = 16
ROPE_THETA = 500000.0
N_RNN_HEADS = 4
RNN_DIM = 128
ATT_WIDTH = N_ATT_HEADS * V_DIM
RNN_WIDTH = N_RNN_HEADS * RNN_DIM
IN_WIDTH = 3 * ATT_WIDTH + 4 * RNN_WIDTH
N_EXPERTS = 8
EPS = 1e-6
NEG = -1e30
LANES = 128
V7X_VMEM_BYTES = 64 * 1024 * 1024
HGRN_CHUNK = 128
HGRN_SUB = 16

NT_DIMS = (((1,), (1,)), ((), ()))


def _params(semantics, vmem_mb):
    return pltpu.CompilerParams(dimension_semantics=semantics,
                                vmem_limit_bytes=vmem_mb * 1024 * 1024)


def _sigmoid(x):
    return 1.0 / (1.0 + jnp.exp(-x))


def _mm(a, b, exact, dims=None):
    if exact:
        a, b, prec = a.astype(F32), b.astype(F32), lax.Precision.HIGHEST
    else:
        a, b, prec = a.astype(BF16), b.astype(BF16), None
    if dims is None:
        return jnp.dot(a, b, precision=prec, preferred_element_type=F32)
    return lax.dot_general(a, b, dims, precision=prec, preferred_element_type=F32)


def _split_bf16(x, parts):
    out = []
    for _ in range(parts - 1):
        h = x.astype(BF16)
        out.append(h)
        x = x - h.astype(F32)
    out.append(x.astype(BF16))
    return out


def _group_mean_sq(x, bd):
    parts = _split_bf16(x * x, 2)
    return (jnp.dot(parts[0], bd, preferred_element_type=F32)
            + jnp.dot(parts[1], bd, preferred_element_type=F32))


def _qk_norm_rope(z, gain, bd, cos, sin_hi, sin_lo):
    r = lax.rsqrt(_group_mean_sq(z, bd) + EPS)
    y = z * r * gain
    half = ROT_DIM // 2
    up = pltpu.roll(y, LANES - half, axis=1)
    dn = pltpu.roll(y, half, axis=1)
    return y * cos + up * sin_hi + dn * sin_lo


def _inproj_kernel(x_ref, g_ref, w_ref, qg_ref, kg_ref, bd_ref, cos_ref, shi_ref, slo_ref,
                   q_ref, k_ref, v_ref, r_ref, kb_ref, vb_ref, *, exact):
    x = x_ref[...]
    r = lax.rsqrt(jnp.mean(x * x, axis=-1, keepdims=True) + EPS)
    hn = x * r * g_ref[...]
    if not exact:
        hn = hn.astype(BF16)
    bd = bd_ref[...]
    cos, shi, slo = cos_ref[...], shi_ref[...], slo_ref[...]
    for h in range(N_ATT_HEADS):
        lo, hi = h * LANES, (h + 1) * LANES
        zq = _mm(hn, w_ref[:, lo:hi], exact)
        q_ref[:, lo:hi] = _qk_norm_rope(zq, qg_ref[...], bd, cos, shi, slo)
        zk = _mm(hn, w_ref[:, ATT_WIDTH + lo:ATT_WIDTH + hi], exact)
        k_head = _qk_norm_rope(zk, kg_ref[...], bd, cos, shi, slo)
        k_ref[:, lo:hi] = k_head
        kb_ref[:, lo:hi] = k_head.astype(BF16)
    v = _mm(hn, w_ref[:, 2 * ATT_WIDTH:3 * ATT_WIDTH], exact)
    v_ref[...] = v
    vb_ref[...] = v.astype(BF16)
    for j in range(4):
        lo, hi = j * RNN_WIDTH, (j + 1) * RNN_WIDTH
        r_ref[:, lo:hi] = _mm(hn, w_ref[:, 3 * ATT_WIDTH + lo:3 * ATT_WIDTH + hi], exact)


def _input_projection(x, ln_g, w, qg, kg, bd, tabs, tm):
    n = x.shape[0]
    cos, shi, slo = tabs
    nt = cos.shape[0] // tm
    row = lambda i: (i, 0)
    fixed = lambda i: (0, 0)
    tab = lambda i: (i % nt, 0)
    return pl.pallas_call(
        functools.partial(_inproj_kernel, exact=w.dtype == F32),
        grid=(n // tm,),
        in_specs=[
            pl.BlockSpec((tm, D_MODEL), row),
            pl.BlockSpec((1, D_MODEL), fixed),
            pl.BlockSpec((D_MODEL, IN_WIDTH), fixed),
            pl.BlockSpec((1, LANES), fixed),
            pl.BlockSpec((1, LANES), fixed),
            pl.BlockSpec((LANES, LANES), fixed),
            pl.BlockSpec((tm, LANES), tab),
            pl.BlockSpec((tm, LANES), tab),
            pl.BlockSpec((tm, LANES), tab),
        ],
        out_specs=[
            pl.BlockSpec((tm, ATT_WIDTH), row),
            pl.BlockSpec((tm, ATT_WIDTH), row),
            pl.BlockSpec((tm, ATT_WIDTH), row),
            pl.BlockSpec((tm, 4 * RNN_WIDTH), row),
            pl.BlockSpec((tm, ATT_WIDTH), row),
            pl.BlockSpec((tm, ATT_WIDTH), row),
        ],
        out_shape=[
            jax.ShapeDtypeStruct((n, ATT_WIDTH), F32),
            jax.ShapeDtypeStruct((n, ATT_WIDTH), F32),
            jax.ShapeDtypeStruct((n, ATT_WIDTH), F32),
            jax.ShapeDtypeStruct((n, 4 * RNN_WIDTH), F32),
            jax.ShapeDtypeStruct((n, ATT_WIDTH), BF16),
            jax.ShapeDtypeStruct((n, ATT_WIDTH), BF16),
        ],
        compiler_params=_params(("parallel",), 48),
        name="input_projection",
    )(x, ln_g, w, qg, kg, bd, cos, shi, slo)


def _rope_tables(start, t):
    half = ROT_DIM // 2
    inv = np.power(ROPE_THETA, -np.arange(half, dtype=np.float64) * 2.0 / ROT_DIM)
    pos = start + np.arange(t)
    coarse = np.arange(pos[0] // LANES, pos[-1] // LANES + 1, dtype=np.float64)[:, None] * LANES * inv
    fine = np.arange(LANES, dtype=np.float64)[:, None] * inv
    const = lambda a: jnp.asarray(a.astype(np.float32))
    a_idx = jnp.asarray(pos // LANES - pos[0] // LANES)
    b_idx = jnp.asarray(pos % LANES)
    ca, sa = const(np.cos(coarse))[a_idx], const(np.sin(coarse))[a_idx]
    cb, sb = const(np.cos(fine))[b_idx], const(np.sin(fine))[b_idx]
    cos, sin = ca * cb - sa * sb, sa * cb + ca * sb
    one = jnp.ones((t, QK_DIM - ROT_DIM), F32)
    zero = jnp.zeros((t, QK_DIM - ROT_DIM), F32)
    zh = jnp.zeros((t, half), F32)
    comp = lambda a, b, rest: jnp.concatenate([a, b, rest], axis=1)
    c64 = comp(cos, cos, one)
    hi64 = comp(-sin, zh, zero)
    lo64 = comp(zh, sin, zero)
    two = lambda a: jnp.concatenate([a, a], axis=1)
    return two(c64), two(hi64), two(lo64)


def _attn_prompt_kernel(lam_ref, q_ref, k_ref, v_ref, g_ref, o_ref, acc, *, tq, tk, lam_scale):
    qi = pl.program_id(2)
    q = q_ref[...] * (QK_DIM ** -0.5)
    lane = lax.broadcasted_iota(jnp.int32, q.shape, 1)
    q_cat = jnp.concatenate([jnp.where(lane < QK_DIM, q, 0.0), jnp.where(lane >= QK_DIM, q, 0.0)],
                            axis=0).astype(BF16)
    acc[...] = jnp.zeros_like(acc)

    def tile(j, carry, diag):
        m, l = carry
        start = pl.multiple_of(j * tk, tk)
        s = lax.dot_general(q_cat, k_ref[pl.ds(start, tk), :], NT_DIMS, preferred_element_type=F32)
        if diag is not None:
            rows = lax.broadcasted_iota(jnp.int32, (2 * tq, tk), 0) % tq
            cols = lax.broadcasted_iota(jnp.int32, (2 * tq, tk), 1) + diag * tk
            s = jnp.where(cols <= rows, s, NEG)
        m_new = jnp.maximum(m, jnp.max(s, axis=-1, keepdims=True))
        alpha = jnp.exp(m - m_new)
        p = jnp.exp(s - m_new)
        l_new = alpha * l + jnp.sum(p, axis=-1, keepdims=True)
        acc[...] = alpha * acc[...] + jnp.dot(p.astype(BF16), v_ref[pl.ds(start, tk), :],
                                              preferred_element_type=F32)
        return m_new, l_new

    per_q = tq // tk
    carry = (jnp.full((2 * tq, 1), NEG, F32), jnp.zeros((2 * tq, 1), F32))
    carry = lax.fori_loop(0, qi * per_q, lambda j, c: tile(j, c, None), carry)
    for d in range(per_q):
        carry = tile(qi * per_q + d, carry, d)
    o2 = acc[...] / carry[1]
    o = o2[:tq] - lam_ref[0] * o2[tq:]
    r = lax.rsqrt(jnp.mean(o * o, axis=-1, keepdims=True) + EPS)
    o_ref[...] = o * r * g_ref[...] * lam_scale


def _attention_prompt(q, k, v, lam, subln_g, lam_init, batch, seq, tq, tk):
    nq = seq // tq
    kernel = functools.partial(_attn_prompt_kernel, tq=tq, tk=tk, lam_scale=1.0 - lam_init)
    return pl.pallas_call(
        kernel,
        grid=(batch, N_ATT_HEADS, nq),
        in_specs=[
            pl.BlockSpec(memory_space=pltpu.SMEM),
            pl.BlockSpec((tq, V_DIM), lambda b, h, i: (b * nq + i, h)),
            pl.BlockSpec((seq, V_DIM), lambda b, h, i: (b, h)),
            pl.BlockSpec((seq, V_DIM), lambda b, h, i: (b, h)),
            pl.BlockSpec((1, V_DIM), lambda b, h, i: (0, 0)),
        ],
        out_specs=pl.BlockSpec((tq, V_DIM), lambda b, h, i: (b * nq + i, h)),
        out_shape=jax.ShapeDtypeStruct((batch * seq, ATT_WIDTH), F32),
        scratch_shapes=[pltpu.VMEM((2 * tq, V_DIM), F32)],
        compiler_params=_params(("parallel", "parallel", "arbitrary"), 32),
        name="attention_prompt",
    )(lam, q, k, v, subln_g)


def _two_term(x):
    return jnp.concatenate(_split_bf16(x, 2), axis=0)


def _attn_decode_kernel(pt_ref, lam_ref, q_ref, kn_ref, vn_ref, *rest, n_pages, lam_scale):
    k_refs, v_refs = rest[:n_pages], rest[n_pages:2 * n_pages]
    g_ref, o_ref = rest[2 * n_pages:]
    rows_per_page = k_refs[0].shape[0]
    n_rows = 2 * N_ATT_HEADS
    lane = lax.broadcasted_iota(jnp.int32, (N_ATT_HEADS, V_DIM), 1)
    q4 = q_ref[0] * (QK_DIM ** -0.5)
    qm = jnp.concatenate([jnp.where(lane < QK_DIM, q4, 0.0), jnp.where(lane >= QK_DIM, q4, 0.0)], axis=0)
    q_cat = _two_term(qm)

    def two_term_dot(lhs_cat, rhs, dims):
        out = sum(lax.dot_general(lhs_cat, part, dims, preferred_element_type=F32)
                  for part in _split_bf16(rhs, 2))
        return out[:n_rows] + out[n_rows:]

    r_i = lax.broadcasted_iota(jnp.int32, (n_rows, rows_per_page), 0)
    c_i = lax.broadcasted_iota(jnp.int32, (n_rows, rows_per_page), 1)
    keep = (c_i % N_ATT_HEADS) == (r_i % N_ATT_HEADS)
    scores = [jnp.where(keep, two_term_dot(q_cat, k_ref[...], NT_DIMS), NEG) for k_ref in k_refs]

    prod = q4 * kn_ref[0]
    s_new = jnp.concatenate([jnp.sum(jnp.where(lane < QK_DIM, prod, 0.0), axis=-1, keepdims=True),
                             jnp.sum(jnp.where(lane >= QK_DIM, prod, 0.0), axis=-1, keepdims=True)], axis=0)
    m = s_new
    for s in scores:
        m = jnp.maximum(m, jnp.max(s, axis=-1, keepdims=True))
    p_new = jnp.exp(s_new - m)
    vn = vn_ref[0]
    l = p_new
    acc = p_new * jnp.concatenate([vn, vn], axis=0)
    for s, v_ref in zip(scores, v_refs):
        p = jnp.exp(s - m)
        l = l + jnp.sum(p, axis=-1, keepdims=True)
        acc = acc + two_term_dot(_two_term(p), v_ref[...], (((1,), (0,)), ((), ())))
    o8 = acc / l
    o = o8[:N_ATT_HEADS] - lam_ref[0] * o8[N_ATT_HEADS:]
    r = lax.rsqrt(jnp.mean(o * o, axis=-1, keepdims=True) + EPS)
    o_ref[0] = o * r * g_ref[...] * lam_scale


def _attention_decode(q, k_new, v_new, cache_k, cache_v, page_table, layer, lam, subln_g, lam_init):
    nb = q.shape[0]
    n_pages = page_table.shape[1]
    rows = cache_k.shape[2]
    heads3 = lambda a: a.reshape(nb, N_ATT_HEADS, V_DIM)
    tok = lambda b, pt: (b, 0, 0)
    page_specs = [pl.BlockSpec((None, None, rows, V_DIM),
                               functools.partial(lambda p, b, pt: (layer, pt[b * n_pages + p], 0, 0), p))
                  for p in range(n_pages)]
    kernel = functools.partial(_attn_decode_kernel, n_pages=n_pages, lam_scale=1.0 - lam_init)
    out = pl.pallas_call(
        kernel,
        grid_spec=pltpu.PrefetchScalarGridSpec(
            num_scalar_prefetch=1,
            grid=(nb,),
            in_specs=[
                pl.BlockSpec(memory_space=pltpu.SMEM),
                pl.BlockSpec((1, N_ATT_HEADS, V_DIM), tok),
                pl.BlockSpec((1, N_ATT_HEADS, V_DIM), tok),
                pl.BlockSpec((1, N_ATT_HEADS, V_DIM), tok),
                *page_specs, *page_specs,
                pl.BlockSpec((1, V_DIM), lambda b, pt: (0, 0)),
            ],
            out_specs=pl.BlockSpec((1, N_ATT_HEADS, V_DIM), tok),
        ),
        out_shape=jax.ShapeDtypeStruct((nb, N_ATT_HEADS, V_DIM), F32),
        compiler_params=_params(("parallel",), 40),
        name="attention_decode",
    )(page_table.reshape(-1), lam, heads3(q), heads3(k_new), heads3(v_new),
      *([cache_k] * n_pages), *([cache_v] * n_pages), subln_g)
    return out.reshape(nb, ATT_WIDTH)


def _hgrn_gates(fx, log_lb, log1m_lb, one_m_lb):
    e = jnp.exp(-jnp.abs(fx))
    log_sig = jnp.minimum(fx, 0.0) - jnp.log(1.0 + e)
    b = log1m_lb + log_sig
    log_f = jnp.maximum(log_lb, b) + jnp.log(1.0 + jnp.exp(-jnp.abs(log_lb - b)))
    k = one_m_lb * (jnp.where(fx >= 0.0, e, 1.0) / (1.0 + e))
    return log_f, k


def _rnn_out(o, gate, norm_g):
    r = lax.rsqrt(jnp.mean(o * o, axis=-1, keepdims=True) + EPS)
    return o * r * norm_g * (gate * _sigmoid(gate))


def _hgrn_prompt_kernel(q_ref, f_ref, i_ref, g_ref, lb_ref, ng_ref, tri_ref, o_ref, s_ref, s_scr):
    c = pl.program_id(1)
    n_chunks = pl.num_programs(1)
    ch, sub = HGRN_CHUNK, HGRN_SUB
    n_sub = ch // sub

    @pl.when(c == 0)
    def _():
        s_scr[...] = jnp.zeros_like(s_scr)

    tri = tri_ref[...]
    row_c = lax.broadcasted_iota(jnp.int32, (ch, RNN_DIM), 0)
    row_s = lax.broadcasted_iota(jnp.int32, (sub, RNN_DIM), 0)
    lane_s = lax.broadcasted_iota(jnp.int32, (sub, ch), 1)
    eye = (lax.broadcasted_iota(jnp.int32, (RNN_DIM, RNN_DIM), 0)
           == lax.broadcasted_iota(jnp.int32, (RNN_DIM, RNN_DIM), 1))

    for h in range(N_RNN_HEADS):
        lo, hi = h * RNN_DIM, (h + 1) * RNN_DIM
        log_f, k = _hgrn_gates(f_ref[:, lo:hi], lb_ref[0:1, lo:hi], lb_ref[1:2, lo:hi], lb_ref[2:3, lo:hi])
        qx = q_ref[:, lo:hi]
        q = qx * _sigmoid(qx)
        v_bf = i_ref[:, lo:hi].astype(BF16)
        b = sum(jnp.dot(tri, part, preferred_element_type=F32) for part in _split_bf16(log_f, 3))
        state = s_scr[h]
        o = jnp.dot((q * jnp.exp(b)).astype(BF16), state.astype(BF16), preferred_element_type=F32)

        a_rows = []
        for blk in range(n_sub):
            r0 = blk * sub
            q_b, b_b, k_b = q[r0:r0 + sub], b[r0:r0 + sub], k[r0:r0 + sub]
            if blk == 0:
                a_blk = jnp.zeros((sub, ch), F32)
            else:
                ref_row = b[r0 - 1:r0]
                q_dec = (q_b * jnp.exp(b_b - ref_row)).astype(BF16)
                k_dec = jnp.where(row_c < r0, k * jnp.exp(jnp.minimum(ref_row - b, 0.0)), 0.0).astype(BF16)
                a_blk = lax.dot_general(q_dec, k_dec, NT_DIMS, preferred_element_type=F32)
            for s in range(sub):
                w = q_b * jnp.exp(jnp.minimum(b_b - b_b[s:s + 1], 0.0)) * k_b[s:s + 1]
                col = jnp.sum(jnp.where(row_s >= s, w, 0.0), axis=-1, keepdims=True)
                a_blk = a_blk + jnp.where(lane_s == r0 + s, col, 0.0)
            a_rows.append(a_blk)
        a = jnp.concatenate(a_rows, axis=0)
        o = o + jnp.dot(a.astype(BF16), v_bf, preferred_element_type=F32)

        b_last = b[ch - 1:ch]
        decay_col = jnp.sum(jnp.where(eye, jnp.exp(b_last), 0.0), axis=1, keepdims=True)
        k_end = (k * jnp.exp(b_last - b)).T.astype(BF16)
        new_state = state * decay_col + jnp.dot(k_end, v_bf, preferred_element_type=F32)
        s_scr[h] = new_state
        o_ref[:, lo:hi] = _rnn_out(o, g_ref[:, lo:hi], ng_ref[...])

        @pl.when(c == n_chunks - 1)
        def _():
            s_ref[0, h] = new_state


def _hgrn_prompt(r, lb_rows, norm_g, tri, batch, seq):
    ch = HGRN_CHUNK
    nc = seq // ch
    col = lambda j: pl.BlockSpec((ch, RNN_WIDTH), lambda b, c: (b * nc + c, j))
    fixed = lambda b, c: (0, 0)
    return pl.pallas_call(
        _hgrn_prompt_kernel,
        grid=(batch, nc),
        in_specs=[col(0), col(1), col(2), col(3),
                  pl.BlockSpec((3, RNN_WIDTH), fixed),
                  pl.BlockSpec((1, RNN_DIM), fixed),
                  pl.BlockSpec((ch, ch), fixed)],
        out_specs=[
            pl.BlockSpec((ch, RNN_WIDTH), lambda b, c: (b * nc + c, 0)),
            pl.BlockSpec((1, N_RNN_HEADS, RNN_DIM, RNN_DIM), lambda b, c: (b, 0, 0, 0)),
        ],
        out_shape=[
            jax.ShapeDtypeStruct((batch * seq, RNN_WIDTH), F32),
            jax.ShapeDtypeStruct((batch, N_RNN_HEADS, RNN_DIM, RNN_DIM), F32),
        ],
        scratch_shapes=[pltpu.VMEM((N_RNN_HEADS, RNN_DIM, RNN_DIM), F32)],
        compiler_params=_params(("parallel", "arbitrary"), 32),
        name="hgrn_prompt",
    )(r, r, r, r, lb_rows, norm_g, tri)


def _hgrn_decode_kernel(r_ref, s_ref, lb_ref, ng_ref, o_ref, so_ref):
    eye = (lax.broadcasted_iota(jnp.int32, (RNN_DIM, RNN_DIM), 0)
           == lax.broadcasted_iota(jnp.int32, (RNN_DIM, RNN_DIM), 1))
    to_col = lambda row: jnp.sum(jnp.where(eye, row, 0.0), axis=1, keepdims=True)
    for h in range(N_RNN_HEADS):
        seg = lambda j: r_ref[0, :, j * RNN_WIDTH + h * RNN_DIM:j * RNN_WIDTH + (h + 1) * RNN_DIM]
        lo, hi = h * RNN_DIM, (h + 1) * RNN_DIM
        log_f, k = _hgrn_gates(seg(1), lb_ref[0:1, lo:hi], lb_ref[1:2, lo:hi], lb_ref[2:3, lo:hi])
        qx = seg(0)
        q = qx * _sigmoid(qx)
        new_state = s_ref[0, h] * to_col(jnp.exp(log_f)) + to_col(k) * seg(2)
        so_ref[0, h] = new_state
        o = jnp.sum(to_col(q) * new_state, axis=0, keepdims=True)
        o_ref[0, :, lo:hi] = _rnn_out(o, seg(3), ng_ref[...])


def _hgrn_decode(r, state, layer, lb_rows, norm_g):
    nb = r.shape[0]
    out, new_state = pl.pallas_call(
        _hgrn_decode_kernel,
        grid=(nb,),
        in_specs=[
            pl.BlockSpec((1, 1, 4 * RNN_WIDTH), lambda b: (b, 0, 0)),
            pl.BlockSpec((None, 1, N_RNN_HEADS, RNN_DIM, RNN_DIM), lambda b: (layer, b, 0, 0, 0)),
            pl.BlockSpec((3, RNN_WIDTH), lambda b: (0, 0)),
            pl.BlockSpec((1, RNN_DIM), lambda b: (0, 0)),
        ],
        out_specs=[
            pl.BlockSpec((1, 1, RNN_WIDTH), lambda b: (b, 0, 0)),
            pl.BlockSpec((1, N_RNN_HEADS, RNN_DIM, RNN_DIM), lambda b: (b, 0, 0, 0)),
        ],
        out_shape=[
            jax.ShapeDtypeStruct((nb, 1, RNN_WIDTH), F32),
            jax.ShapeDtypeStruct((nb, N_RNN_HEADS, RNN_DIM, RNN_DIM), F32),
        ],
        compiler_params=_params(("parallel",), 32),
        name="hgrn_decode",
    )(r.reshape(nb, 1, 4 * RNN_WIDTH), state, lb_rows, norm_g)
    return out.reshape(nb, RNN_WIDTH), new_state


def _outproj_kernel(x_ref, att_ref, rnn_ref, w_ref, g_ref, *rest, routed, exact):
    if routed:
        wr_ref, xo_ref, h_ref, comb_ref = rest
    else:
        xo_ref, h_ref = rest
    mix = (_mm(att_ref[...], w_ref[:ATT_WIDTH, :], exact)
           + _mm(rnn_ref[...], w_ref[ATT_WIDTH:, :], exact))
    x = x_ref[...] + mix
    xo_ref[...] = x
    r = lax.rsqrt(jnp.mean(x * x, axis=-1, keepdims=True) + EPS)
    hf = x * r * g_ref[...]
    h_ref[...] = hf.astype(h_ref.dtype)
    if routed:
        logits = _mm(hf, wr_ref[...], True)
        lane = lax.broadcasted_iota(jnp.int32, logits.shape, 1)
        lane_f = lane.astype(F32)
        logits = jnp.where(lane < N_EXPERTS, logits, -jnp.inf)
        v1 = jnp.max(logits, axis=-1, keepdims=True)
        i1 = jnp.min(jnp.where(logits == v1, lane_f, float(LANES)), axis=-1, keepdims=True)
        first = lane_f == i1
        rest_logits = jnp.where(first, -jnp.inf, logits)
        v2 = jnp.max(rest_logits, axis=-1, keepdims=True)
        i2 = jnp.min(jnp.where(rest_logits == v2, lane_f, float(LANES)), axis=-1, keepdims=True)
        second = lane_f == i2
        e = jnp.exp(v2 - v1)
        g1 = 1.0 / (1.0 + e)
        picked = (lane_f - N_EXPERTS == i1) | (lane_f - N_EXPERTS == i2)
        comb_ref[...] = (jnp.where(first, g1, 0.0) + jnp.where(second, e * g1, 0.0)
                         + jnp.where(picked, 1.0, 0.0))


def _output_projection(x, att, rnn, w, ln_g, router, tm):
    exact = w.dtype == F32
    n = x.shape[0]
    row = lambda i: (i, 0)
    fixed = lambda i: (0, 0)
    in_specs = [
        pl.BlockSpec((tm, D_MODEL), row),
        pl.BlockSpec((tm, ATT_WIDTH), row),
        pl.BlockSpec((tm, RNN_WIDTH), row),
        pl.BlockSpec((D_MODEL, D_MODEL), fixed),
        pl.BlockSpec((1, D_MODEL), fixed),
    ]
    out_specs = [pl.BlockSpec((tm, D_MODEL), row), pl.BlockSpec((tm, D_MODEL), row)]
    out_shape = [jax.ShapeDtypeStruct((n, D_MODEL), F32), jax.ShapeDtypeStruct((n, D_MODEL), F32 if exact else BF16)]
    args = [x, att, rnn, w, ln_g]
    if router is not None:
        in_specs.append(pl.BlockSpec((D_MODEL, LANES), fixed))
        out_specs.append(pl.BlockSpec((tm, LANES), row))
        out_shape.append(jax.ShapeDtypeStruct((n, LANES), F32))
        args.append(router)
    return pl.pallas_call(
        functools.partial(_outproj_kernel, routed=router is not None, exact=exact),
        grid=(n // tm,),
        in_specs=in_specs,
        out_specs=out_specs,
        out_shape=out_shape,
        compiler_params=_params(("parallel",), 32),
        name="output_projection",
    )(*args)


def _ffn_kernel(x_ref, h_ref, wg_ref, wu_ref, wd_ref, y_ref, acc, *, exact):
    f = pl.program_id(1)

    @pl.when(f == 0)
    def _():
        acc[...] = jnp.zeros_like(acc)

    h = h_ref[...]
    g = _mm(h, wg_ref[...], exact)
    u = _mm(h, wu_ref[...], exact)
    acc[...] += _mm(g * _sigmoid(g) * u, wd_ref[...], exact)

    @pl.when(f == pl.num_programs(1) - 1)
    def _():
        y_ref[...] = x_ref[...] + acc[...]


def _dense_ffn(x, h, w_gu, w_dn, tm, tf):
    n = x.shape[0]
    d_ff = w_dn.shape[0]
    nf = d_ff // tf
    return pl.pallas_call(
        functools.partial(_ffn_kernel, exact=w_gu.dtype == F32),
        grid=(n // tm, nf),
        in_specs=[
            pl.BlockSpec((tm, D_MODEL), lambda i, f: (i, 0)),
            pl.BlockSpec((tm, D_MODEL), lambda i, f: (i, 0)),
            pl.BlockSpec((D_MODEL, tf), lambda i, f: (0, f)),
            pl.BlockSpec((D_MODEL, tf), lambda i, f: (0, f + nf)),
            pl.BlockSpec((tf, D_MODEL), lambda i, f: (f, 0)),
        ],
        out_specs=pl.BlockSpec((tm, D_MODEL), lambda i, f: (i, 0)),
        out_shape=jax.ShapeDtypeStruct((n, D_MODEL), F32),
        scratch_shapes=[pltpu.VMEM((tm, D_MODEL), F32)],
        compiler_params=_params(("parallel", "arbitrary"), 48),
        name="dense_ffn",
    )(x, h, w_gu, w_gu, w_dn)


def _moe_kernel(x_ref, h_ref, comb_ref, wg_ref, wu_ref, wd_ref, y_ref, acc):
    e = pl.program_id(1)
    f = pl.program_id(2)

    @pl.when((e == 0) & (f == 0))
    def _():
        acc[...] = jnp.zeros_like(acc)

    comb = comb_ref[...]
    lane = lax.broadcasted_iota(jnp.int32, comb.shape, 1)
    gate = jnp.sum(jnp.where(lane == e, comb, 0.0), axis=-1, keepdims=True)
    h = h_ref[...]
    g = jnp.dot(h, wg_ref[...], preferred_element_type=F32)
    u = jnp.dot(h, wu_ref[...], preferred_element_type=F32)
    a = (g * _sigmoid(g) * u).astype(BF16)
    acc[...] += gate * jnp.dot(a, wd_ref[...], preferred_element_type=F32)

    @pl.when((e == pl.num_programs(1) - 1) & (f == pl.num_programs(2) - 1))
    def _():
        y_ref[...] = x_ref[...] + acc[...]


def _moe_ffn(x, h, comb, w_gu, w_dn, tm, tf):
    n = x.shape[0]
    d_ff = w_dn.shape[1]
    nf = d_ff // tf
    return pl.pallas_call(
        _moe_kernel,
        grid=(n // tm, N_EXPERTS, nf),
        in_specs=[
            pl.BlockSpec((tm, D_MODEL), lambda i, e, f: (i, 0)),
            pl.BlockSpec((tm, D_MODEL), lambda i, e, f: (i, 0)),
            pl.BlockSpec((tm, LANES), lambda i, e, f: (i, 0)),
            pl.BlockSpec((None, D_MODEL, tf), lambda i, e, f: (e, 0, f)),
            pl.BlockSpec((None, D_MODEL, tf), lambda i, e, f: (e, 0, f + nf)),
            pl.BlockSpec((None, tf, D_MODEL), lambda i, e, f: (e, f, 0)),
        ],
        out_specs=pl.BlockSpec((tm, D_MODEL), lambda i, e, f: (i, 0)),
        out_shape=jax.ShapeDtypeStruct((n, D_MODEL), F32),
        scratch_shapes=[pltpu.VMEM((tm, D_MODEL), F32)],
        compiler_params=_params(("parallel", "arbitrary", "arbitrary"), 48),
        name="moe_ffn",
    )(x, h, comb, w_gu, w_gu, w_dn)


MOE_ROWS = 512
MOE_TOKS = 512


def _route(comb, n_tok):
    rb, tb = MOE_ROWS, MOE_TOKS
    i32 = jnp.int32
    nb_max = 2 * n_tok // rb + N_EXPERTS
    n_sub = n_tok // tb
    sel = comb[:, N_EXPERTS:2 * N_EXPERTS] > 0.5
    csum = jnp.cumsum(sel.astype(i32), axis=0)
    cnt = csum[-1]
    nblk = (cnt + rb - 1) // rb
    blk_end = jnp.cumsum(nblk)
    blk_off = blk_end - nblk
    dest = jnp.where(sel, blk_off[None, :] * rb + csum - 1, -1)

    blk = jnp.arange(nb_max, dtype=i32)
    blk_e = jnp.minimum(jnp.searchsorted(blk_end, blk, side="right"), N_EXPERTS - 1).astype(i32)
    blk_ok = blk < blk_end[-1]
    k = blk - blk_off[blk_e]
    csum_t = csum.T
    first_with = jax.vmap(lambda e, r: jnp.searchsorted(csum_t[e], r, side="left"))
    j_lo = jnp.where(blk_ok, first_with(blk_e, k * rb + 1) // tb, 0).astype(i32)
    j_hi = jnp.where(blk_ok, first_with(blk_e, jnp.minimum((k + 1) * rb, cnt[blk_e])) // tb, 0).astype(i32)
    n_pair = j_hi - j_lo + 1
    pair_end = jnp.cumsum(n_pair)
    pair_off = pair_end - n_pair
    n_pairs_max = nb_max + N_EXPERTS * n_sub
    i = jnp.arange(n_pairs_max, dtype=i32)
    live = i < pair_end[-1]
    p_blk = jnp.minimum(jnp.searchsorted(pair_end, i, side="right"), nb_max - 1).astype(i32)
    p_j = jnp.where(live, j_lo[p_blk] + i - pair_off[p_blk], j_hi[nb_max - 1]).astype(i32)
    p_flags = ((live & (i == pair_off[p_blk])).astype(i32)
               + 2 * (live & blk_ok[p_blk]).astype(i32))
    gather_tabs = (p_blk, p_j, p_flags, blk_e[p_blk])

    ends = csum[tb - 1::tb]
    starts = jnp.concatenate([jnp.zeros((1, N_EXPERTS), i32), ends[:-1]], axis=0)
    has = ends > starts
    w0 = jnp.minimum((blk_off[None, :] * rb + starts) // rb, nb_max - 1)
    w1 = jnp.where(has, (blk_off[None, :] * rb + ends - 1) // rb, w0)
    win_blk = jnp.stack([w0, w1], axis=-1).reshape(-1).astype(i32)
    win_ok = jnp.stack([has, has & (w1 > w0)], axis=-1).reshape(-1).astype(i32)
    return dest, gather_tabs, (blk_e, blk_ok.astype(i32)), (win_blk, win_ok), nb_max


def _moe_gather_kernel(blk_ref, j_ref, flag_ref, e_ref, h_ref, dest_ref, o_ref):
    i = pl.program_id(0)
    flags = flag_ref[i]

    @pl.when(flags % 2 == 1)
    def _():
        o_ref[...] = jnp.zeros_like(o_ref)

    @pl.when(flags >= 2)
    def _():
        d = dest_ref[...]
        sub = lax.broadcasted_iota(jnp.int32, d.shape, 0)
        d_e = jnp.sum(jnp.where(sub == e_ref[i], d, 0.0), axis=0, keepdims=True)
        rows = lax.broadcasted_iota(jnp.int32, (MOE_ROWS, MOE_TOKS), 0).astype(F32)
        base = (blk_ref[i] * MOE_ROWS).astype(F32)
        onehot = jnp.where(d_e - base == rows, 1.0, 0.0).astype(BF16)
        o_ref[...] = (o_ref[...].astype(F32)
                      + jnp.dot(onehot, h_ref[...], preferred_element_type=F32)).astype(BF16)


def _moe_group_kernel(e_ref, ok_ref, h_ref, wg_ref, wu_ref, wd_ref, o_ref, acc):
    b = pl.program_id(0)
    f = pl.program_id(1)

    @pl.when(f == 0)
    def _():
        acc[...] = jnp.zeros_like(acc)

    @pl.when(ok_ref[b] == 1)
    def _():
        h = h_ref[...]
        g = jnp.dot(h, wg_ref[...], preferred_element_type=F32)
        u = jnp.dot(h, wu_ref[...], preferred_element_type=F32)
        acc[...] += jnp.dot((g * _sigmoid(g) * u).astype(BF16), wd_ref[...], preferred_element_type=F32)

    @pl.when(f == pl.num_programs(1) - 1)
    def _():
        o_ref[...] = acc[...].astype(BF16)


def _moe_combine_kernel(wblk_ref, wok_ref, x_ref, comb_ref, dest_ref, rows_ref, y_ref, acc):
    i, e, w = pl.program_id(0), pl.program_id(1), pl.program_id(2)
    idx = (i * N_EXPERTS + e) * 2 + w

    @pl.when((e == 0) & (w == 0))
    def _():
        acc[...] = jnp.zeros_like(acc)

    @pl.when(wok_ref[idx] == 1)
    def _():
        lane = lax.broadcasted_iota(jnp.int32, comb_ref.shape, 1)
        mine = lane == e
        d = jnp.sum(jnp.where(mine, dest_ref[...], 0.0), axis=-1, keepdims=True)
        gate = jnp.sum(jnp.where(mine, comb_ref[...], 0.0), axis=-1, keepdims=True)
        cols = lax.broadcasted_iota(jnp.int32, (MOE_TOKS, MOE_ROWS), 1).astype(F32)
        base = (wblk_ref[idx] * MOE_ROWS).astype(F32)
        onehot = jnp.where(d - base == cols, 1.0, 0.0).astype(BF16)
        acc[...] += gate * jnp.dot(onehot, rows_ref[...], preferred_element_type=F32)

    @pl.when((e == N_EXPERTS - 1) & (w == 1))
    def _():
        y_ref[...] = x_ref[...] + acc[...]


def _moe_routed(x, h, comb, w_gu, w_dn, tf):
    n = x.shape[0]
    rb, tb = MOE_ROWS, MOE_TOKS
    d_ff = w_dn.shape[1]
    nf = d_ff // tf
    dest, gather_tabs, (blk_e, blk_ok), (win_blk, win_ok), nb_max = _route(comb, n)
    dest_f = dest.astype(F32)
    dest_rows = dest_f.T
    dest_lanes = jnp.pad(dest_f, ((0, 0), (0, LANES - N_EXPERTS)), constant_values=-1.0)

    n_pairs = gather_tabs[0].shape[0]
    h_sorted = pl.pallas_call(
        _moe_gather_kernel,
        grid_spec=pltpu.PrefetchScalarGridSpec(
            num_scalar_prefetch=4,
            grid=(n_pairs,),
            in_specs=[
                pl.BlockSpec((tb, D_MODEL), lambda i, blk, j, fl, e: (j[i], 0)),
                pl.BlockSpec((N_EXPERTS, tb), lambda i, blk, j, fl, e: (0, j[i])),
            ],
            out_specs=pl.BlockSpec((rb, D_MODEL), lambda i, blk, j, fl, e: (blk[i], 0)),
        ),
        out_shape=jax.ShapeDtypeStruct((nb_max * rb, D_MODEL), BF16),
        compiler_params=_params(("arbitrary",), 32),
        name="moe_gather",
    )(*gather_tabs, h, dest_rows)

    frozen = lambda f, ok: f * ok + (nf - 1) * (1 - ok)
    y_sorted = pl.pallas_call(
        _moe_group_kernel,
        grid_spec=pltpu.PrefetchScalarGridSpec(
            num_scalar_prefetch=2,
            grid=(nb_max, nf),
            in_specs=[
                pl.BlockSpec((rb, D_MODEL), lambda b, f, e, ok: (b, 0)),
                pl.BlockSpec((None, D_MODEL, tf), lambda b, f, e, ok: (e[b], 0, frozen(f, ok[b]))),
                pl.BlockSpec((None, D_MODEL, tf), lambda b, f, e, ok: (e[b], 0, frozen(f, ok[b]) + nf)),
                pl.BlockSpec((None, tf, D_MODEL), lambda b, f, e, ok: (e[b], frozen(f, ok[b]), 0)),
            ],
            out_specs=pl.BlockSpec((rb, D_MODEL), lambda b, f, e, ok: (b, 0)),
            scratch_shapes=[pltpu.VMEM((rb, D_MODEL), F32)],
        ),
        out_shape=jax.ShapeDtypeStruct((nb_max * rb, D_MODEL), BF16),
        compiler_params=_params(("parallel", "arbitrary"), 48),
        name="moe_experts",
    )(blk_e, blk_ok, h_sorted, w_gu, w_gu, w_dn)

    tok = lambda i, e, w, wb, ok: (i, 0)
    return pl.pallas_call(
        _moe_combine_kernel,
        grid_spec=pltpu.PrefetchScalarGridSpec(
            num_scalar_prefetch=2,
            grid=(n // tb, N_EXPERTS, 2),
            in_specs=[
                pl.BlockSpec((tb, D_MODEL), tok),
                pl.BlockSpec((tb, LANES), tok),
                pl.BlockSpec((tb, LANES), tok),
                pl.BlockSpec((rb, D_MODEL), lambda i, e, w, wb, ok: (wb[(i * N_EXPERTS + e) * 2 + w], 0)),
            ],
            out_specs=pl.BlockSpec((tb, D_MODEL), tok),
            scratch_shapes=[pltpu.VMEM((tb, D_MODEL), F32)],
        ),
        out_shape=jax.ShapeDtypeStruct((n, D_MODEL), F32),
        compiler_params=_params(("parallel", "arbitrary", "arbitrary"), 32),
        name="moe_combine",
    )(win_blk, win_ok, x, comb, dest_lanes, y_sorted)


def _row(a):
    return a.reshape(1, -1).astype(F32)


def kernel(x_prompt, x_sample, cache_k, cache_v, state_hgrn, page_table, ln1_g, w_in, q_norm_g, k_norm_g,
           lam_p, subln_g, hgrn_lb, hgrn_norm_g, w_out, ln2_g, w_ffn_gu, w_ffn_dn, w_router, w_exp_gu,
           w_exp_dn):
    bp, tp, _ = x_prompt.shape
    bs = x_sample.shape[0]
    depth = w_in.shape[0]
    n_pool, page_size = cache_k.shape[1], cache_k.shape[2]
    past = page_table.shape[1] * page_size

    xp = x_prompt.reshape(bp * tp, D_MODEL)
    xs = x_sample.reshape(bs, D_MODEL)
    ck = cache_k.reshape(depth, n_pool, page_size * N_ATT_HEADS, V_DIM)
    cv = cache_v.reshape(depth, n_pool, page_size * N_ATT_HEADS, V_DIM)

    tabs_p = _rope_tables(0, tp)
    tabs_s = [jnp.broadcast_to(tab, (bs, LANES)) for tab in _rope_tables(past, 1)]
    lane = jnp.arange(LANES)
    bd = jnp.where((lane[:, None] // QK_DIM) == (lane[None, :] // QK_DIM), 1.0 / QK_DIM, 0.0).astype(BF16)
    tok = jnp.arange(HGRN_CHUNK)
    tri = (tok[None, :] <= tok[:, None]).astype(BF16)

    lbs = jnp.cumsum(jax.nn.softmax(hgrn_lb.astype(F32), axis=0), axis=0)
    lbs = lbs - lbs[0]

    n_p = bp * tp
    tm_p = min(256, n_p)
    tq = min(512, tp)
    tk = min(512, tq)
    tm_ffn = min(1024, n_p)
    outs = {name: [] for name in ("kp", "vp", "sp", "ks", "vs", "ss")}
    for l in range(depth):
        lam_init = 0.8 - 0.6 * math.exp(-0.3 * l)
        lp = lam_p[l].astype(F32)
        lam = (jnp.exp(jnp.sum(lp[0] * lp[1])) - jnp.exp(jnp.sum(lp[2] * lp[3])) + lam_init).reshape(1)
        w_in_bf = w_in[l].astype(BF16)
        w_out_bf = w_out[l].astype(BF16)
        qg, kg = _row(q_norm_g[l]), _row(k_norm_g[l])
        lb_rows = jnp.stack([jnp.log(lbs[l]), jnp.log1p(-lbs[l]), 1.0 - lbs[l]])
        sub_g, rnn_g = _row(subln_g[l]), _row(hgrn_norm_g[l])
        ln1, ln2 = _row(ln1_g[l]), _row(ln2_g[l])
        router = None
        if l % 2 == 1:
            router = jnp.pad(w_router[l // 2].astype(F32), ((0, 0), (0, LANES - N_EXPERTS)))

        q, k, v, r, k_bf, v_bf = _input_projection(xp, ln1, w_in_bf, qg, kg, bd, tabs_p, tm_p)
        att = _attention_prompt(q, k_bf, v_bf, lam, sub_g, lam_init, bp, tp, tq, tk)
        rnn, s_fin = _hgrn_prompt(r, lb_rows, rnn_g, tri, bp, tp)
        proj_p = _output_projection(xp, att, rnn, w_out_bf, ln2, router, tm_p)
        outs["kp"].append(k.reshape(bp, tp, N_ATT_HEADS, V_DIM))
        outs["vp"].append(v.reshape(bp, tp, N_ATT_HEADS, V_DIM))
        outs["sp"].append(s_fin)

        q, k, v, r, _, _ = _input_projection(xs, ln1, w_in[l], qg, kg, bd, tabs_s, bs)
        att = _attention_decode(q, k, v, ck, cv, page_table, l, lam, sub_g, lam_init)
        rnn, s_new = _hgrn_decode(r, state_hgrn, l, lb_rows, rnn_g)
        proj_s = _output_projection(xs, att, rnn, w_out[l], ln2, router, bs)
        outs["ks"].append(k.reshape(bs, 1, N_ATT_HEADS, V_DIM))
        outs["vs"].append(v.reshape(bs, 1, N_ATT_HEADS, V_DIM))
        outs["ss"].append(s_new)

        if l % 2 == 0:
            w_gu = w_ffn_gu[l // 2].astype(BF16)
            w_dn = w_ffn_dn[l // 2].astype(BF16)
            xp = _dense_ffn(proj_p[0], proj_p[1], w_gu, w_dn, tm_ffn, 1408)
            xs = _dense_ffn(proj_s[0], proj_s[1], w_ffn_gu[l // 2], w_ffn_dn[l // 2], bs, 256)
        else:
            w_gu = w_exp_gu[l // 2].astype(BF16)
            w_dn = w_exp_dn[l // 2].astype(BF16)
            xp = _moe_routed(proj_p[0], proj_p[1], proj_p[2], w_gu, w_dn, 896)
            xs = _moe_ffn(proj_s[0], proj_s[1].astype(BF16), proj_s[2], w_gu, w_dn, bs, 896)

    return (xp.reshape(bp, tp, D_MODEL), xs.reshape(bs, 1, D_MODEL),
            jnp.stack(outs["kp"]), jnp.stack(outs["vp"]), jnp.stack(outs["sp"]),
            jnp.stack(outs["ks"]), jnp.stack(outs["vs"]), jnp.stack(outs["ss"]))
```

```python
import functools
import math

import jax
import jax.numpy as jnp
import numpy as np
from jax import lax
from jax.experimental import pallas as pl
from jax.experimental.pallas import tpu as pltpu

F32 = jnp.float32
BF16 = jnp.bfloat16

D_MODEL = 1024
N_ATT_HEADS = 4
V_DIM = 128
QK_DIM = 64
ROT_DIM = 16
ROPE_THETA = 500000.0
N_RNN_HEADS = 4
RNN_DIM = 128
ATT_WIDTH = N_ATT_HEADS * V_DIM
RNN_WIDTH = N_RNN_HEADS * RNN_DIM
IN_WIDTH = 3 * ATT_WIDTH + 4 * RNN_WIDTH
N_EXPERTS = 8
EPS = 1e-6
NEG = -1e30
LANES = 128
V7X_VMEM_BYTES = 64 * 1024 * 1024
HGRN_CHUNK = 128
HGRN_SUB = 16

NT_DIMS = (((1,), (1,)), ((), ()))


def _params(semantics, vmem_mb):
    return pltpu.CompilerParams(dimension_semantics=semantics,
                                vmem_limit_bytes=vmem_mb * 1024 * 1024)


def _sigmoid(x):
    return 1.0 / (1.0 + jnp.exp(-x))


def _mm(a, b, exact, dims=None):
    if exact:
        a, b, prec = a.astype(F32), b.astype(F32), lax.Precision.HIGHEST
    else:
        a, b, prec = a.astype(BF16), b.astype(BF16), None
    if dims is None:
        return jnp.dot(a, b, precision=prec, preferred_element_type=F32)
    return lax.dot_general(a, b, dims, precision=prec, preferred_element_type=F32)


def _split_bf16(x, parts):
    out = []
    for _ in range(parts - 1):
        h = x.astype(BF16)
        out.append(h)
        x = x - h.astype(F32)
    out.append(x.astype(BF16))
    return out


def _group_mean_sq(x, bd):
    parts = _split_bf16(x * x, 2)
    return (jnp.dot(parts[0], bd, preferred_element_type=F32)
            + jnp.dot(parts[1], bd, preferred_element_type=F32))


def _qk_norm_rope(z, gain, bd, cos, sin_hi, sin_lo):
    r = lax.rsqrt(_group_mean_sq(z, bd) + EPS)
    y = z * r * gain
    half = ROT_DIM // 2
    up = pltpu.roll(y, LANES - half, axis=1)
    dn = pltpu.roll(y, half, axis=1)
    return y * cos + up * sin_hi + dn * sin_lo


def _inproj_kernel(x_ref, g_ref, w_ref, qg_ref, kg_ref, bd_ref, cos_ref, shi_ref, slo_ref,
                   q_ref, k_ref, v_ref, r_ref, kb_ref, vb_ref, *, exact):
    x = x_ref[...]
    r = lax.rsqrt(jnp.mean(x * x, axis=-1, keepdims=True) + EPS)
    hn = x * r * g_ref[...]
    if not exact:
        hn = hn.astype(BF16)
    bd = bd_ref[...]
    cos, shi, slo = cos_ref[...], shi_ref[...], slo_ref[...]
    for h in range(N_ATT_HEADS):
        lo, hi = h * LANES, (h + 1) * LANES
        zq = _mm(hn, w_ref[:, lo:hi], exact)
        q_ref[:, lo:hi] = _qk_norm_rope(zq, qg_ref[...], bd, cos, shi, slo)
        zk = _mm(hn, w_ref[:, ATT_WIDTH + lo:ATT_WIDTH + hi], exact)
        k_head = _qk_norm_rope(zk, kg_ref[...], bd, cos, shi, slo)
        k_ref[:, lo:hi] = k_head
        kb_ref[:, lo:hi] = k_head.astype(BF16)
    v = _mm(hn, w_ref[:, 2 * ATT_WIDTH:3 * ATT_WIDTH], exact)
    v_ref[...] = v
    vb_ref[...] = v.astype(BF16)
    for j in range(4):
        lo, hi = j * RNN_WIDTH, (j + 1) * RNN_WIDTH
        r_ref[:, lo:hi] = _mm(hn, w_ref[:, 3 * ATT_WIDTH + lo:3 * ATT_WIDTH + hi], exact)


def _input_projection(x, ln_g, w, qg, kg, bd, tabs, tm):
    n = x.shape[0]
    cos, shi, slo = tabs
    nt = cos.shape[0] // tm
    row = lambda i: (i, 0)
    fixed = lambda i: (0, 0)
    tab = lambda i: (i % nt, 0)
    return pl.pallas_call(
        functools.partial(_inproj_kernel, exact=w.dtype == F32),
        grid=(n // tm,),
        in_specs=[
            pl.BlockSpec((tm, D_MODEL), row),
            pl.BlockSpec((1, D_MODEL), fixed),
            pl.BlockSpec((D_MODEL, IN_WIDTH), fixed),
            pl.BlockSpec((1, LANES), fixed),
            pl.BlockSpec((1, LANES), fixed),
            pl.BlockSpec((LANES, LANES), fixed),
            pl.BlockSpec((tm, LANES), tab),
            pl.BlockSpec((tm, LANES), tab),
            pl.BlockSpec((tm, LANES), tab),
        ],
        out_specs=[
            pl.BlockSpec((tm, ATT_WIDTH), row),
            pl.BlockSpec((tm, ATT_WIDTH), row),
            pl.BlockSpec((tm, ATT_WIDTH), row),
            pl.BlockSpec((tm, 4 * RNN_WIDTH), row),
            pl.BlockSpec((tm, ATT_WIDTH), row),
            pl.BlockSpec((tm, ATT_WIDTH), row),
        ],
        out_shape=[
            jax.ShapeDtypeStruct((n, ATT_WIDTH), F32),
            jax.ShapeDtypeStruct((n, ATT_WIDTH), F32),
            jax.ShapeDtypeStruct((n, ATT_WIDTH), F32),
            jax.ShapeDtypeStruct((n, 4 * RNN_WIDTH), F32),
            jax.ShapeDtypeStruct((n, ATT_WIDTH), BF16),
            jax.ShapeDtypeStruct((n, ATT_WIDTH), BF16),
        ],
        compiler_params=_params(("parallel",), 48),
        name="input_projection",
    )(x, ln_g, w, qg, kg, bd, cos, shi, slo)


def _rope_tables(start, t):
    half = ROT_DIM // 2
    inv = np.power(ROPE_THETA, -np.arange(half, dtype=np.float64) * 2.0 / ROT_DIM)
    pos = start + np.arange(t)
    coarse = np.arange(pos[0] // LANES, pos[-1] // LANES + 1, dtype=np.float64)[:, None] * LANES * inv
    fine = np.arange(LANES, dtype=np.float64)[:, None] * inv
    const = lambda a: jnp.asarray(a.astype(np.float32))
    a_idx = jnp.asarray(pos // LANES - pos[0] // LANES)
    b_idx = jnp.asarray(pos % LANES)
    ca, sa = const(np.cos(coarse))[a_idx], const(np.sin(coarse))[a_idx]
    cb, sb = const(np.cos(fine))[b_idx], const(np.sin(fine))[b_idx]
    cos, sin = ca * cb - sa * sb, sa * cb + ca * sb
    one = jnp.ones((t, QK_DIM - ROT_DIM), F32)
    zero = jnp.zeros((t, QK_DIM - ROT_DIM), F32)
    zh = jnp.zeros((t, half), F32)
    comp = lambda a, b, rest: jnp.concatenate([a, b, rest], axis=1)
    c64 = comp(cos, cos, one)
    hi64 = comp(-sin, zh, zero)
    lo64 = comp(zh, sin, zero)
    two = lambda a: jnp.concatenate([a, a], axis=1)
    return two(c64), two(hi64), two(lo64)


def _attn_prompt_kernel(lam_ref, q_ref, k_ref, v_ref, g_ref, o_ref, acc, m_scr, a_scr, s_scr, p_scr, *,
                        tq, lam_scale):
    tk = tq
    qi = pl.program_id(2)
    q = q_ref[...] * (QK_DIM ** -0.5)
    lane = lax.broadcasted_iota(jnp.int32, q.shape, 1)
    q_cat = jnp.concatenate([jnp.where(lane < QK_DIM, q, 0.0), jnp.where(lane >= QK_DIM, q, 0.0)],
                            axis=0).astype(BF16)
    acc[...] = jnp.zeros_like(acc)
    m_scr[...] = jnp.full_like(m_scr, NEG)
    a_scr[...] = jnp.ones_like(a_scr)
    p_scr[1] = jnp.zeros(p_scr.shape[1:], BF16)

    def scores(j, slot):
        start = pl.multiple_of(j * tk, tk)
        s_scr[slot] = lax.dot_general(q_cat, k_ref[pl.ds(start, tk), :], NT_DIMS, preferred_element_type=F32)

    def softmax(slot, diagonal):
        s = s_scr[slot]
        if diagonal:
            rows = lax.broadcasted_iota(jnp.int32, (2 * tq, tk), 0) % tq
            cols = lax.broadcasted_iota(jnp.int32, (2 * tq, tk), 1)
            s = jnp.where(cols <= rows, s, NEG)
        m = m_scr[...]
        m_new = jnp.maximum(m, jnp.max(s, axis=-1, keepdims=True))
        a_scr[...] = jnp.exp(m - m_new)
        m_scr[...] = m_new
        p_scr[slot] = jnp.exp(s - m_new).astype(BF16)

    def update(j, slot):
        start = pl.multiple_of(j * tk, tk)
        v_ones = jnp.concatenate([v_ref[pl.ds(start, tk), :], jnp.ones((tk, V_DIM), BF16)], axis=1)
        acc[...] = a_scr[...] * acc[...] + jnp.dot(p_scr[slot], v_ones, preferred_element_type=F32)

    def step(j, slot, diagonal):
        if not diagonal:
            scores(j + 1, 1 - slot)
        update(jnp.maximum(j - 1, 0), 1 - slot)
        softmax(slot, diagonal)
        if diagonal:
            update(j, slot)

    def by_parity(j, diagonal):
        for slot in range(2):
            pl.when(j % 2 == slot)(functools.partial(step, j, slot, diagonal))

    scores(0, 0)

    def body(j, carry):
        by_parity(j, False)
        return carry

    lax.fori_loop(0, qi, body, 0)
    by_parity(qi, True)
    o2 = acc[:, :V_DIM] / acc[:, V_DIM:]
    o = o2[:tq] - lam_ref[0] * o2[tq:]
    r = lax.rsqrt(jnp.mean(o * o, axis=-1, keepdims=True) + EPS)
    o_ref[...] = o * r * g_ref[...] * lam_scale


def _attention_prompt(q, k, v, lam, subln_g, lam_init, batch, seq, tq):
    nq = seq // tq
    kernel = functools.partial(_attn_prompt_kernel, tq=tq, lam_scale=1.0 - lam_init)
    return pl.pallas_call(
        kernel,
        grid=(batch, N_ATT_HEADS, nq),
        in_specs=[
            pl.BlockSpec(memory_space=pltpu.SMEM),
            pl.BlockSpec((tq, V_DIM), lambda b, h, i: (b * nq + i, h)),
            pl.BlockSpec((seq, V_DIM), lambda b, h, i: (b, h)),
            pl.BlockSpec((seq, V_DIM), lambda b, h, i: (b, h)),
            pl.BlockSpec((1, V_DIM), lambda b, h, i: (0, 0)),
        ],
        out_specs=pl.BlockSpec((tq, V_DIM), lambda b, h, i: (b * nq + i, h)),
        out_shape=jax.ShapeDtypeStruct((batch * seq, ATT_WIDTH), F32),
        scratch_shapes=[pltpu.VMEM((2 * tq, 2 * V_DIM), F32),
                        pltpu.VMEM((2 * tq, 1), F32),
                        pltpu.VMEM((2 * tq, 1), F32),
                        pltpu.VMEM((2, 2 * tq, tq), F32),
                        pltpu.VMEM((2, 2 * tq, tq), BF16)],
        compiler_params=_params(("parallel", "parallel", "arbitrary"), 40),
        name="attention_prompt",
    )(lam, q, k, v, subln_g)


def _two_term(x):
    return jnp.concatenate(_split_bf16(x, 2), axis=0)


def _attn_decode_kernel(pt_ref, lam_ref, q_ref, kn_ref, vn_ref, *rest, n_pages, lam_scale):
    k_refs, v_refs = rest[:n_pages], rest[n_pages:2 * n_pages]
    g_ref, o_ref = rest[2 * n_pages:]
    rows_per_page = k_refs[0].shape[0]
    n_rows = 2 * N_ATT_HEADS
    lane = lax.broadcasted_iota(jnp.int32, (N_ATT_HEADS, V_DIM), 1)
    q4 = q_ref[0] * (QK_DIM ** -0.5)
    qm = jnp.concatenate([jnp.where(lane < QK_DIM, q4, 0.0), jnp.where(lane >= QK_DIM, q4, 0.0)], axis=0)
    q_cat = _two_term(qm)

    def two_term_dot(lhs_cat, rhs, dims):
        out = sum(lax.dot_general(lhs_cat, part, dims, preferred_element_type=F32)
                  for part in _split_bf16(rhs, 2))
        return out[:n_rows] + out[n_rows:]

    r_i = lax.broadcasted_iota(jnp.int32, (n_rows, rows_per_page), 0)
    c_i = lax.broadcasted_iota(jnp.int32, (n_rows, rows_per_page), 1)
    keep = (c_i % N_ATT_HEADS) == (r_i % N_ATT_HEADS)
    scores = [jnp.where(keep, two_term_dot(q_cat, k_ref[...], NT_DIMS), NEG) for k_ref in k_refs]

    prod = q4 * kn_ref[0]
    s_new = jnp.concatenate([jnp.sum(jnp.where(lane < QK_DIM, prod, 0.0), axis=-1, keepdims=True),
                             jnp.sum(jnp.where(lane >= QK_DIM, prod, 0.0), axis=-1, keepdims=True)], axis=0)
    m = s_new
    for s in scores:
        m = jnp.maximum(m, jnp.max(s, axis=-1, keepdims=True))
    p_new = jnp.exp(s_new - m)
    vn = vn_ref[0]
    l = p_new
    acc = p_new * jnp.concatenate([vn, vn], axis=0)
    for s, v_ref in zip(scores, v_refs):
        p = jnp.exp(s - m)
        l = l + jnp.sum(p, axis=-1, keepdims=True)
        acc = acc + two_term_dot(_two_term(p), v_ref[...], (((1,), (0,)), ((), ())))
    o8 = acc / l
    o = o8[:N_ATT_HEADS] - lam_ref[0] * o8[N_ATT_HEADS:]
    r = lax.rsqrt(jnp.mean(o * o, axis=-1, keepdims=True) + EPS)
    o_ref[0] = o * r * g_ref[...] * lam_scale


def _attention_decode(q, k_new, v_new, cache_k, cache_v, page_table, layer, lam, subln_g, lam_init):
    nb = q.shape[0]
    n_pages = page_table.shape[1]
    rows = cache_k.shape[2]
    heads3 = lambda a: a.reshape(nb, N_ATT_HEADS, V_DIM)
    tok = lambda b, pt: (b, 0, 0)
    page_specs = [pl.BlockSpec((None, None, rows, V_DIM),
                               functools.partial(lambda p, b, pt: (layer, pt[b * n_pages + p], 0, 0), p))
                  for p in range(n_pages)]
    kernel = functools.partial(_attn_decode_kernel, n_pages=n_pages, lam_scale=1.0 - lam_init)
    out = pl.pallas_call(
        kernel,
        grid_spec=pltpu.PrefetchScalarGridSpec(
            num_scalar_prefetch=1,
            grid=(nb,),
            in_specs=[
                pl.BlockSpec(memory_space=pltpu.SMEM),
                pl.BlockSpec((1, N_ATT_HEADS, V_DIM), tok),
                pl.BlockSpec((1, N_ATT_HEADS, V_DIM), tok),
                pl.BlockSpec((1, N_ATT_HEADS, V_DIM), tok),
                *page_specs, *page_specs,
                pl.BlockSpec((1, V_DIM), lambda b, pt: (0, 0)),
            ],
            out_specs=pl.BlockSpec((1, N_ATT_HEADS, V_DIM), tok),
        ),
        out_shape=jax.ShapeDtypeStruct((nb, N_ATT_HEADS, V_DIM), F32),
        compiler_params=_params(("parallel",), 40),
        name="attention_decode",
    )(page_table.reshape(-1), lam, heads3(q), heads3(k_new), heads3(v_new),
      *([cache_k] * n_pages), *([cache_v] * n_pages), subln_g)
    return out.reshape(nb, ATT_WIDTH)


def _hgrn_gates(fx, log_lb, log1m_lb, one_m_lb):
    e = jnp.exp(-jnp.abs(fx))
    log_sig = jnp.minimum(fx, 0.0) - jnp.log(1.0 + e)
    b = log1m_lb + log_sig
    log_f = jnp.maximum(log_lb, b) + jnp.log(1.0 + jnp.exp(-jnp.abs(log_lb - b)))
    k = one_m_lb * (jnp.where(fx >= 0.0, e, 1.0) / (1.0 + e))
    return log_f, k


def _rnn_out(o, gate, norm_g):
    r = lax.rsqrt(jnp.mean(o * o, axis=-1, keepdims=True) + EPS)
    return o * r * norm_g * (gate * _sigmoid(gate))


def _hgrn_prompt_kernel(q_ref, f_ref, i_ref, g_ref, lb_ref, ng_ref, tri_ref, o_ref, s_ref, s_scr):
    c = pl.program_id(1)
    n_chunks = pl.num_programs(1)
    ch, sub = HGRN_CHUNK, HGRN_SUB
    n_sub = ch // sub

    @pl.when(c == 0)
    def _():
        s_scr[...] = jnp.zeros_like(s_scr)

    tri = tri_ref[...]
    row_c = lax.broadcasted_iota(jnp.int32, (ch, RNN_DIM), 0)
    row_s = lax.broadcasted_iota(jnp.int32, (sub, RNN_DIM), 0)
    lane_s = lax.broadcasted_iota(jnp.int32, (sub, ch), 1)
    eye = (lax.broadcasted_iota(jnp.int32, (RNN_DIM, RNN_DIM), 0)
           == lax.broadcasted_iota(jnp.int32, (RNN_DIM, RNN_DIM), 1))

    for h in range(N_RNN_HEADS):
        lo, hi = h * RNN_DIM, (h + 1) * RNN_DIM
        log_f, k = _hgrn_gates(f_ref[:, lo:hi], lb_ref[0:1, lo:hi], lb_ref[1:2, lo:hi], lb_ref[2:3, lo:hi])
        qx = q_ref[:, lo:hi]
        q = qx * _sigmoid(qx)
        v_bf = i_ref[:, lo:hi].astype(BF16)
        b = sum(jnp.dot(tri, part, preferred_element_type=F32) for part in _split_bf16(log_f, 3))
        state = s_scr[h]
        o = jnp.dot((q * jnp.exp(b)).astype(BF16), state.astype(BF16), preferred_element_type=F32)

        a_rows = []
        for blk in range(n_sub):
            r0 = blk * sub
            q_b, b_b, k_b = q[r0:r0 + sub], b[r0:r0 + sub], k[r0:r0 + sub]
            if blk == 0:
                a_blk = jnp.zeros((sub, ch), F32)
            else:
                ref_row = b[r0 - 1:r0]
                q_dec = (q_b * jnp.exp(b_b - ref_row)).astype(BF16)
                k_dec = jnp.where(row_c < r0, k * jnp.exp(jnp.minimum(ref_row - b, 0.0)), 0.0).astype(BF16)
                a_blk = lax.dot_general(q_dec, k_dec, NT_DIMS, preferred_element_type=F32)
            for s in range(sub):
                w = q_b * jnp.exp(b_b - b_b[s:s + 1]) * k_b[s:s + 1]
                col = jnp.sum(jnp.where(row_s >= s, w, 0.0), axis=-1, keepdims=True)
                a_blk = jnp.where(lane_s == r0 + s, col, a_blk)
            a_rows.append(a_blk)
        a = jnp.concatenate(a_rows, axis=0)
        o = o + jnp.dot(a.astype(BF16), v_bf, preferred_element_type=F32)

        b_last = b[ch - 1:ch]
        decay_col = jnp.sum(jnp.where(eye, jnp.exp(b_last), 0.0), axis=1, keepdims=True)
        k_end = (k * jnp.exp(b_last - b)).T.astype(BF16)
        new_state = state * decay_col + jnp.dot(k_end, v_bf, preferred_element_type=F32)
        s_scr[h] = new_state
        o_ref[:, lo:hi] = _rnn_out(o, g_ref[:, lo:hi], ng_ref[...])

        @pl.when(c == n_chunks - 1)
        def _():
            s_ref[0, h] = new_state


def _hgrn_prompt(r, lb_rows, norm_g, tri, batch, seq):
    ch = HGRN_CHUNK
    nc = seq // ch
    col = lambda j: pl.BlockSpec((ch, RNN_WIDTH), lambda b, c: (b * nc + c, j))
    fixed = lambda b, c: (0, 0)
    return pl.pallas_call(
        _hgrn_prompt_kernel,
        grid=(batch, nc),
        in_specs=[col(0), col(1), col(2), col(3),
                  pl.BlockSpec((3, RNN_WIDTH), fixed),
                  pl.BlockSpec((1, RNN_DIM), fixed),
                  pl.BlockSpec((ch, ch), fixed)],
        out_specs=[
            pl.BlockSpec((ch, RNN_WIDTH), lambda b, c: (b * nc + c, 0)),
            pl.BlockSpec((1, N_RNN_HEADS, RNN_DIM, RNN_DIM), lambda b, c: (b, 0, 0, 0)),
        ],
        out_shape=[
            jax.ShapeDtypeStruct((batch * seq, RNN_WIDTH), F32),
            jax.ShapeDtypeStruct((batch, N_RNN_HEADS, RNN_DIM, RNN_DIM), F32),
        ],
        scratch_shapes=[pltpu.VMEM((N_RNN_HEADS, RNN_DIM, RNN_DIM), F32)],
        compiler_params=_params(("parallel", "arbitrary"), 32),
        name="hgrn_prompt",
    )(r, r, r, r, lb_rows, norm_g, tri)


def _hgrn_decode_kernel(r_ref, s_ref, lb_ref, ng_ref, o_ref, so_ref):
    eye = (lax.broadcasted_iota(jnp.int32, (RNN_DIM, RNN_DIM), 0)
           == lax.broadcasted_iota(jnp.int32, (RNN_DIM, RNN_DIM), 1))
    to_col = lambda row: jnp.sum(jnp.where(eye, row, 0.0), axis=1, keepdims=True)
    for h in range(N_RNN_HEADS):
        seg = lambda j: r_ref[0, :, j * RNN_WIDTH + h * RNN_DIM:j * RNN_WIDTH + (h + 1) * RNN_DIM]
        lo, hi = h * RNN_DIM, (h + 1) * RNN_DIM
        log_f, k = _hgrn_gates(seg(1), lb_ref[0:1, lo:hi], lb_ref[1:2, lo:hi], lb_ref[2:3, lo:hi])
        qx = seg(0)
        q = qx * _sigmoid(qx)
        new_state = s_ref[0, h] * to_col(jnp.exp(log_f)) + to_col(k) * seg(2)
        so_ref[0, h] = new_state
        o = jnp.sum(to_col(q) * new_state, axis=0, keepdims=True)
        o_ref[0, :, lo:hi] = _rnn_out(o, seg(3), ng_ref[...])


def _hgrn_decode(r, state, layer, lb_rows, norm_g):
    nb = r.shape[0]
    out, new_state = pl.pallas_call(
        _hgrn_decode_kernel,
        grid=(nb,),
        in_specs=[
            pl.BlockSpec((1, 1, 4 * RNN_WIDTH), lambda b: (b, 0, 0)),
            pl.BlockSpec((None, 1, N_RNN_HEADS, RNN_DIM, RNN_DIM), lambda b: (layer, b, 0, 0, 0)),
            pl.BlockSpec((3, RNN_WIDTH), lambda b: (0, 0)),
            pl.BlockSpec((1, RNN_DIM), lambda b: (0, 0)),
        ],
        out_specs=[
            pl.BlockSpec((1, 1, RNN_WIDTH), lambda b: (b, 0, 0)),
            pl.BlockSpec((1, N_RNN_HEADS, RNN_DIM, RNN_DIM), lambda b: (b, 0, 0, 0)),
        ],
        out_shape=[
            jax.ShapeDtypeStruct((nb, 1, RNN_WIDTH), F32),
            jax.ShapeDtypeStruct((nb, N_RNN_HEADS, RNN_DIM, RNN_DIM), F32),
        ],
        compiler_params=_params(("parallel",), 32),
        name="hgrn_decode",
    )(r.reshape(nb, 1, 4 * RNN_WIDTH), state, lb_rows, norm_g)
    return out.reshape(nb, RNN_WIDTH), new_state


def _outproj_kernel(x_ref, att_ref, rnn_ref, w_ref, g_ref, *rest, routed, exact):
    if routed:
        wr_ref, xo_ref, h_ref, comb_ref = rest
    else:
        xo_ref, h_ref = rest
    mix = (_mm(att_ref[...], w_ref[:ATT_WIDTH, :], exact)
           + _mm(rnn_ref[...], w_ref[ATT_WIDTH:, :], exact))
    x = x_ref[...] + mix
    xo_ref[...] = x
    r = lax.rsqrt(jnp.mean(x * x, axis=-1, keepdims=True) + EPS)
    hf = x * r * g_ref[...]
    h_ref[...] = hf.astype(h_ref.dtype)
    if routed:
        logits = _mm(hf, wr_ref[...], True)
        lane = lax.broadcasted_iota(jnp.int32, logits.shape, 1)
        lane_f = lane.astype(F32)
        logits = jnp.where(lane < N_EXPERTS, logits, -jnp.inf)
        v1 = jnp.max(logits, axis=-1, keepdims=True)
        i1 = jnp.min(jnp.where(logits == v1, lane_f, float(LANES)), axis=-1, keepdims=True)
        first = lane_f == i1
        rest_logits = jnp.where(first, -jnp.inf, logits)
        v2 = jnp.max(rest_logits, axis=-1, keepdims=True)
        i2 = jnp.min(jnp.where(rest_logits == v2, lane_f, float(LANES)), axis=-1, keepdims=True)
        second = lane_f == i2
        e = jnp.exp(v2 - v1)
        g1 = 1.0 / (1.0 + e)
        picked = (lane_f - N_EXPERTS == i1) | (lane_f - N_EXPERTS == i2)
        comb_ref[...] = (jnp.where(first, g1, 0.0) + jnp.where(second, e * g1, 0.0)
                         + jnp.where(picked, 1.0, 0.0))


def _output_projection(x, att, rnn, w, ln_g, router, tm):
    exact = w.dtype == F32
    n = x.shape[0]
    row = lambda i: (i, 0)
    fixed = lambda i: (0, 0)
    in_specs = [
        pl.BlockSpec((tm, D_MODEL), row),
        pl.BlockSpec((tm, ATT_WIDTH), row),
        pl.BlockSpec((tm, RNN_WIDTH), row),
        pl.BlockSpec((D_MODEL, D_MODEL), fixed),
        pl.BlockSpec((1, D_MODEL), fixed),
    ]
    out_specs = [pl.BlockSpec((tm, D_MODEL), row), pl.BlockSpec((tm, D_MODEL), row)]
    out_shape = [jax.ShapeDtypeStruct((n, D_MODEL), F32), jax.ShapeDtypeStruct((n, D_MODEL), F32 if exact else BF16)]
    args = [x, att, rnn, w, ln_g]
    if router is not None:
        in_specs.append(pl.BlockSpec((D_MODEL, LANES), fixed))
        out_specs.append(pl.BlockSpec((tm, LANES), row))
        out_shape.append(jax.ShapeDtypeStruct((n, LANES), F32))
        args.append(router)
    return pl.pallas_call(
        functools.partial(_outproj_kernel, routed=router is not None, exact=exact),
        grid=(n // tm,),
        in_specs=in_specs,
        out_specs=out_specs,
        out_shape=out_shape,
        compiler_params=_params(("parallel",), 32),
        name="output_projection",
    )(*args)


def _ffn_kernel(x_ref, h_ref, wg_ref, wu_ref, wd_ref, y_ref, acc, *, exact):
    f = pl.program_id(1)

    @pl.when(f == 0)
    def _():
        acc[...] = jnp.zeros_like(acc)

    h = h_ref[...]
    g = _mm(h, wg_ref[...], exact)
    u = _mm(h, wu_ref[...], exact)
    acc[...] += _mm(g * _sigmoid(g) * u, wd_ref[...], exact)

    @pl.when(f == pl.num_programs(1) - 1)
    def _():
        y_ref[...] = x_ref[...] + acc[...]


def _dense_ffn(x, h, w_gu, w_dn, tm, tf):
    n = x.shape[0]
    d_ff = w_dn.shape[0]
    nf = d_ff // tf
    return pl.pallas_call(
        functools.partial(_ffn_kernel, exact=w_gu.dtype == F32),
        grid=(n // tm, nf),
        in_specs=[
            pl.BlockSpec((tm, D_MODEL), lambda i, f: (i, 0)),
            pl.BlockSpec((tm, D_MODEL), lambda i, f: (i, 0)),
            pl.BlockSpec((D_MODEL, tf), lambda i, f: (0, f)),
            pl.BlockSpec((D_MODEL, tf), lambda i, f: (0, f + nf)),
            pl.BlockSpec((tf, D_MODEL), lambda i, f: (f, 0)),
        ],
        out_specs=pl.BlockSpec((tm, D_MODEL), lambda i, f: (i, 0)),
        out_shape=jax.ShapeDtypeStruct((n, D_MODEL), F32),
        scratch_shapes=[pltpu.VMEM((tm, D_MODEL), F32)],
        compiler_params=_params(("parallel", "arbitrary"), 48),
        name="dense_ffn",
    )(x, h, w_gu, w_gu, w_dn)


def _moe_kernel(x_ref, h_ref, comb_ref, wg_ref, wu_ref, wd_ref, y_ref, acc):
    e = pl.program_id(1)
    f = pl.program_id(2)

    @pl.when((e == 0) & (f == 0))
    def _():
        acc[...] = jnp.zeros_like(acc)

    comb = comb_ref[...]
    lane = lax.broadcasted_iota(jnp.int32, comb.shape, 1)
    gate = jnp.sum(jnp.where(lane == e, comb, 0.0), axis=-1, keepdims=True)
    h = h_ref[...]
    g = jnp.dot(h, wg_ref[...], preferred_element_type=F32)
    u = jnp.dot(h, wu_ref[...], preferred_element_type=F32)
    a = (g * _sigmoid(g) * u).astype(BF16)
    acc[...] += gate * jnp.dot(a, wd_ref[...], preferred_element_type=F32)

    @pl.when((e == pl.num_programs(1) - 1) & (f == pl.num_programs(2) - 1))
    def _():
        y_ref[...] = x_ref[...] + acc[...]


def _moe_ffn(x, h, comb, w_gu, w_dn, tm, tf):
    n = x.shape[0]
    d_ff = w_dn.shape[1]
    nf = d_ff // tf
    return pl.pallas_call(
        _moe_kernel,
        grid=(n // tm, N_EXPERTS, nf),
        in_specs=[
            pl.BlockSpec((tm, D_MODEL), lambda i, e, f: (i, 0)),
            pl.BlockSpec((tm, D_MODEL), lambda i, e, f: (i, 0)),
            pl.BlockSpec((tm, LANES), lambda i, e, f: (i, 0)),
            pl.BlockSpec((None, D_MODEL, tf), lambda i, e, f: (e, 0, f)),
            pl.BlockSpec((None, D_MODEL, tf), lambda i, e, f: (e, 0, f + nf)),
            pl.BlockSpec((None, tf, D_MODEL), lambda i, e, f: (e, f, 0)),
        ],
        out_specs=pl.BlockSpec((tm, D_MODEL), lambda i, e, f: (i, 0)),
        out_shape=jax.ShapeDtypeStruct((n, D_MODEL), F32),
        scratch_shapes=[pltpu.VMEM((tm, D_MODEL), F32)],
        compiler_params=_params(("parallel", "arbitrary", "arbitrary"), 48),
        name="moe_ffn",
    )(x, h, comb, w_gu, w_gu, w_dn)


MOE_ROWS = 512
MOE_TOKS = 512
MOE_WIN = 256


def _route(comb, n_tok):
    rb, tb = MOE_ROWS, MOE_TOKS
    i32 = jnp.int32
    nb_max = 2 * n_tok // rb + N_EXPERTS
    n_sub = n_tok // tb
    sel = comb[:, N_EXPERTS:2 * N_EXPERTS] > 0.5
    csum = jnp.cumsum(sel.astype(i32), axis=0)
    cnt = csum[-1]
    nblk = (cnt + rb - 1) // rb
    blk_end = jnp.cumsum(nblk)
    blk_off = blk_end - nblk
    dest = jnp.where(sel, blk_off[None, :] * rb + csum - 1, -1)

    blk = jnp.arange(nb_max, dtype=i32)
    count_le = lambda ends, x: jnp.sum((ends[None, :] <= x[:, None]).astype(i32), axis=1)
    blk_e = jnp.minimum(count_le(blk_end, blk), N_EXPERTS - 1)
    blk_ok = blk < blk_end[-1]
    k = blk - blk_off[blk_e]
    csum_blk = csum.T[blk_e]
    first_with = lambda _, r: jnp.sum((csum_blk < r[:, None]).astype(i32), axis=1)
    j_lo = jnp.where(blk_ok, first_with(blk_e, k * rb + 1) // tb, 0).astype(i32)
    j_hi = jnp.where(blk_ok, first_with(blk_e, jnp.minimum((k + 1) * rb, cnt[blk_e])) // tb, 0).astype(i32)
    n_pair = j_hi - j_lo + 1
    pair_end = jnp.cumsum(n_pair)
    pair_off = pair_end - n_pair
    n_pairs_max = nb_max + N_EXPERTS * n_sub
    i = jnp.arange(n_pairs_max, dtype=i32)
    live = i < pair_end[-1]
    p_blk = jnp.minimum(count_le(pair_end, i), nb_max - 1)
    p_j = jnp.where(live, j_lo[p_blk] + i - pair_off[p_blk], j_hi[nb_max - 1]).astype(i32)
    p_flags = ((live & (i == pair_off[p_blk])).astype(i32)
               + 2 * (live & blk_ok[p_blk]).astype(i32))
    gather_tabs = (p_blk, p_j, p_flags, blk_e[p_blk])

    win = MOE_WIN
    ends = csum[tb - 1::tb]
    starts = jnp.concatenate([jnp.zeros((1, N_EXPERTS), i32), ends[:-1]], axis=0)
    has = (ends > starts).reshape(-1)
    c_lo = ((blk_off[None, :] * rb + starts) // win).reshape(-1)
    c_hi = ((blk_off[None, :] * rb + ends - 1) // win).reshape(-1)
    n_win = jnp.where(has, c_hi - c_lo + 1, 0)
    win_end = jnp.cumsum(n_win)
    win_off = win_end - n_win
    n_win_max = nb_max * (rb // win) + N_EXPERTS * n_sub
    w = jnp.arange(n_win_max, dtype=i32)
    w_live = w < win_end[-1]
    w_pair = jnp.minimum(count_le(win_end, w), n_sub * N_EXPERTS - 1)
    w_blk = jnp.where(w_live, c_lo[w_pair] + w - win_off[w_pair], 0).astype(i32)
    w_tok = w_pair // N_EXPERTS
    tok_first = jnp.concatenate([jnp.zeros((1,), i32), win_end[N_EXPERTS - 1::N_EXPERTS]])
    w_flags = ((w_live & (w == tok_first[w_tok])).astype(i32)
               + 2 * (w_live & (w == tok_first[w_tok + 1] - 1)).astype(i32)
               + 4 * w_live.astype(i32))
    combine_tabs = (w_tok.astype(i32), (w_pair % N_EXPERTS).astype(i32), w_blk, w_flags)
    return dest, gather_tabs, (blk_e, blk_ok.astype(i32)), combine_tabs, nb_max


def _moe_gather_kernel(blk_ref, j_ref, flag_ref, e_ref, h_ref, dest_ref, o_ref):
    i = pl.program_id(0)
    flags = flag_ref[i]

    @pl.when(flags % 2 == 1)
    def _():
        o_ref[...] = jnp.zeros_like(o_ref)

    @pl.when(flags >= 2)
    def _():
        d = dest_ref[...]
        sub = lax.broadcasted_iota(jnp.int32, d.shape, 0)
        d_e = jnp.sum(jnp.where(sub == e_ref[i], d, 0.0), axis=0, keepdims=True)
        rows = lax.broadcasted_iota(jnp.int32, (MOE_ROWS, MOE_TOKS), 0).astype(F32)
        base = (blk_ref[i] * MOE_ROWS).astype(F32)
        onehot = jnp.where(d_e - base == rows, 1.0, 0.0).astype(BF16)
        o_ref[...] = (o_ref[...].astype(F32)
                      + jnp.dot(onehot, h_ref[...], preferred_element_type=F32)).astype(BF16)


def _moe_group_kernel(e_ref, ok_ref, h_ref, wg_ref, wu_ref, wd_ref, o_ref, acc):
    b = pl.program_id(0)
    f = pl.program_id(1)

    @pl.when(f == 0)
    def _():
        acc[...] = jnp.zeros_like(acc)

    @pl.when(ok_ref[b] == 1)
    def _():
        h = h_ref[...]
        g = jnp.dot(h, wg_ref[...], preferred_element_type=F32)
        u = jnp.dot(h, wu_ref[...], preferred_element_type=F32)
        acc[...] += jnp.dot((g * _sigmoid(g) * u).astype(BF16), wd_ref[...], preferred_element_type=F32)

    @pl.when(f == pl.num_programs(1) - 1)
    def _():
        o_ref[...] = acc[...].astype(BF16)


def _moe_combine_kernel(tok_ref, e_ref, blk_ref, flag_ref, x_ref, comb_ref, dest_ref, rows_ref, y_ref, acc):
    i = pl.program_id(0)
    flags = flag_ref[i]

    @pl.when(flags % 2 == 1)
    def _():
        acc[...] = jnp.zeros_like(acc)

    @pl.when(flags >= 4)
    def _():
        lane = lax.broadcasted_iota(jnp.int32, comb_ref.shape, 1)
        mine = lane == e_ref[i]
        d = jnp.sum(jnp.where(mine, dest_ref[...], 0.0), axis=-1, keepdims=True)
        gate = jnp.sum(jnp.where(mine, comb_ref[...], 0.0), axis=-1, keepdims=True)
        cols = lax.broadcasted_iota(jnp.int32, (MOE_TOKS, MOE_WIN), 1).astype(F32)
        base = (blk_ref[i] * MOE_WIN).astype(F32)
        onehot = jnp.where(d - base == cols, 1.0, 0.0).astype(BF16)
        acc[...] += gate * jnp.dot(onehot, rows_ref[...], preferred_element_type=F32)

    @pl.when((flags // 2) % 2 == 1)
    def _():
        y_ref[...] = x_ref[...] + acc[...]


def _moe_routed(x, h, comb, w_gu, w_dn, tf):
    n = x.shape[0]
    rb, tb = MOE_ROWS, MOE_TOKS
    d_ff = w_dn.shape[1]
    nf = d_ff // tf
    dest, gather_tabs, (blk_e, blk_ok), combine_tabs, nb_max = _route(comb, n)
    dest_f = dest.astype(F32)
    dest_rows = dest_f.T
    dest_lanes = jnp.pad(dest_f, ((0, 0), (0, LANES - N_EXPERTS)), constant_values=-1.0)

    n_pairs = gather_tabs[0].shape[0]
    h_sorted = pl.pallas_call(
        _moe_gather_kernel,
        grid_spec=pltpu.PrefetchScalarGridSpec(
            num_scalar_prefetch=4,
            grid=(n_pairs,),
            in_specs=[
                pl.BlockSpec((tb, D_MODEL), lambda i, blk, j, fl, e: (j[i], 0)),
                pl.BlockSpec((N_EXPERTS, tb), lambda i, blk, j, fl, e: (0, j[i])),
            ],
            out_specs=pl.BlockSpec((rb, D_MODEL), lambda i, blk, j, fl, e: (blk[i], 0)),
        ),
        out_shape=jax.ShapeDtypeStruct((nb_max * rb, D_MODEL), BF16),
        compiler_params=_params(("arbitrary",), 32),
        name="moe_gather",
    )(*gather_tabs, h, dest_rows)

    frozen = lambda f, ok: f * ok + (nf - 1) * (1 - ok)
    y_sorted = pl.pallas_call(
        _moe_group_kernel,
        grid_spec=pltpu.PrefetchScalarGridSpec(
            num_scalar_prefetch=2,
            grid=(nb_max, nf),
            in_specs=[
                pl.BlockSpec((rb, D_MODEL), lambda b, f, e, ok: (b, 0)),
                pl.BlockSpec((None, D_MODEL, tf), lambda b, f, e, ok: (e[b], 0, frozen(f, ok[b]))),
                pl.BlockSpec((None, D_MODEL, tf), lambda b, f, e, ok: (e[b], 0, frozen(f, ok[b]) + nf)),
                pl.BlockSpec((None, tf, D_MODEL), lambda b, f, e, ok: (e[b], frozen(f, ok[b]), 0)),
            ],
            out_specs=pl.BlockSpec((rb, D_MODEL), lambda b, f, e, ok: (b, 0)),
            scratch_shapes=[pltpu.VMEM((rb, D_MODEL), F32)],
        ),
        out_shape=jax.ShapeDtypeStruct((nb_max * rb, D_MODEL), BF16),
        compiler_params=_params(("parallel", "arbitrary"), 48),
        name="moe_experts",
    )(blk_e, blk_ok, h_sorted, w_gu, w_gu, w_dn)

    tok = lambda i, t, e, blk, fl: (t[i], 0)
    return pl.pallas_call(
        _moe_combine_kernel,
        grid_spec=pltpu.PrefetchScalarGridSpec(
            num_scalar_prefetch=4,
            grid=(combine_tabs[0].shape[0],),
            in_specs=[
                pl.BlockSpec((tb, D_MODEL), tok),
                pl.BlockSpec((tb, LANES), tok),
                pl.BlockSpec((tb, LANES), tok),
                pl.BlockSpec((MOE_WIN, D_MODEL), lambda i, t, e, blk, fl: (blk[i], 0)),
            ],
            out_specs=pl.BlockSpec((tb, D_MODEL), tok),
            scratch_shapes=[pltpu.VMEM((tb, D_MODEL), F32)],
        ),
        out_shape=jax.ShapeDtypeStruct((n, D_MODEL), F32),
        compiler_params=_params(("arbitrary",), 32),
        name="moe_combine",
    )(*combine_tabs, x, comb, dest_lanes, y_sorted)


def _row(a):
    return a.reshape(1, -1).astype(F32)


def kernel(x_prompt, x_sample, cache_k, cache_v, state_hgrn, page_table, ln1_g, w_in, q_norm_g, k_norm_g,
           lam_p, subln_g, hgrn_lb, hgrn_norm_g, w_out, ln2_g, w_ffn_gu, w_ffn_dn, w_router, w_exp_gu,
           w_exp_dn):
    bp, tp, _ = x_prompt.shape
    bs = x_sample.shape[0]
    depth = w_in.shape[0]
    n_pool, page_size = cache_k.shape[1], cache_k.shape[2]
    past = page_table.shape[1] * page_size

    xp = x_prompt.reshape(bp * tp, D_MODEL)
    xs = x_sample.reshape(bs, D_MODEL)
    ck = cache_k.reshape(depth, n_pool, page_size * N_ATT_HEADS, V_DIM)
    cv = cache_v.reshape(depth, n_pool, page_size * N_ATT_HEADS, V_DIM)

    tabs_p = _rope_tables(0, tp)
    tabs_s = [jnp.broadcast_to(tab, (bs, LANES)) for tab in _rope_tables(past, 1)]
    lane = jnp.arange(LANES)
    bd = jnp.where((lane[:, None] // QK_DIM) == (lane[None, :] // QK_DIM), 1.0 / QK_DIM, 0.0).astype(BF16)
    tok = jnp.arange(HGRN_CHUNK)
    tri = (tok[None, :] <= tok[:, None]).astype(BF16)

    lbs = jnp.cumsum(jax.nn.softmax(hgrn_lb.astype(F32), axis=0), axis=0)
    lbs = lbs - lbs[0]

    n_p = bp * tp
    tm_p = min(256, n_p)
    tq = min(512, tp)
    tm_ffn = min(1024, n_p)
    outs = {name: [] for name in ("kp", "vp", "sp", "ks", "vs", "ss")}
    for l in range(depth):
        lam_init = 0.8 - 0.6 * math.exp(-0.3 * l)
        lp = lam_p[l].astype(F32)
        lam = (jnp.exp(jnp.sum(lp[0] * lp[1])) - jnp.exp(jnp.sum(lp[2] * lp[3])) + lam_init).reshape(1)
        w_in_bf = w_in[l].astype(BF16)
        w_out_bf = w_out[l].astype(BF16)
        qg, kg = _row(q_norm_g[l]), _row(k_norm_g[l])
        lb_rows = jnp.stack([jnp.log(lbs[l]), jnp.log1p(-lbs[l]), 1.0 - lbs[l]])
        sub_g, rnn_g = _row(subln_g[l]), _row(hgrn_norm_g[l])
        ln1, ln2 = _row(ln1_g[l]), _row(ln2_g[l])
        router = None
        if l % 2 == 1:
            router = jnp.pad(w_router[l // 2].astype(F32), ((0, 0), (0, LANES - N_EXPERTS)))

        q, k, v, r, k_bf, v_bf = _input_projection(xp, ln1, w_in_bf, qg, kg, bd, tabs_p, tm_p)
        att = _attention_prompt(q, k_bf, v_bf, lam, sub_g, lam_init, bp, tp, tq)
        rnn, s_fin = _hgrn_prompt(r, lb_rows, rnn_g, tri, bp, tp)
        proj_p = _output_projection(xp, att, rnn, w_out_bf, ln2, router, tm_p)
        outs["kp"].append(k.reshape(bp, tp, N_ATT_HEADS, V_DIM))
        outs["vp"].append(v.reshape(bp, tp, N_ATT_HEADS, V_DIM))
        outs["sp"].append(s_fin)

        q, k, v, r, _, _ = _input_projection(xs, ln1, w_in[l], qg, kg, bd, tabs_s, bs)
        att = _attention_decode(q, k, v, ck, cv, page_table, l, lam, sub_g, lam_init)
        rnn, s_new = _hgrn_decode(r, state_hgrn, l, lb_rows, rnn_g)
        proj_s = _output_projection(xs, att, rnn, w_out[l], ln2, router, bs)
        outs["ks"].append(k.reshape(bs, 1, N_ATT_HEADS, V_DIM))
        outs["vs"].append(v.reshape(bs, 1, N_ATT_HEADS, V_DIM))
        outs["ss"].append(s_new)

        if l % 2 == 0:
            w_gu = w_ffn_gu[l // 2].astype(BF16)
            w_dn = w_ffn_dn[l // 2].astype(BF16)
            xp = _dense_ffn(proj_p[0], proj_p[1], w_gu, w_dn, tm_ffn, 1408)
            xs = _dense_ffn(proj_s[0], proj_s[1], w_ffn_gu[l // 2], w_ffn_dn[l // 2], bs, 256)
        else:
            w_gu = w_exp_gu[l // 2].astype(BF16)
            w_dn = w_exp_dn[l // 2].astype(BF16)
            xp = _moe_routed(proj_p[0], proj_p[1], proj_p[2], w_gu, w_dn, 896)
            xs = _moe_ffn(proj_s[0], proj_s[1].astype(BF16), proj_s[2], w_gu, w_dn, bs, 896)

    return (xp.reshape(bp, tp, D_MODEL), xs.reshape(bs, 1, D_MODEL),
            jnp.stack(outs["kp"]), jnp.stack(outs["vp"]), jnp.stack(outs["sp"]),
            jnp.stack(outs["ks"]), jnp.stack(outs["vs"]), jnp.stack(outs["ss"]))
```

```python
import functools
import math

import jax
import jax.numpy as jnp
import numpy as np
from jax import lax
from jax.experimental import pallas as pl
from jax.experimental.pallas import tpu as pltpu

F32 = jnp.float32
BF16 = jnp.bfloat16

D_MODEL = 1024
N_ATT_HEADS = 4
V_DIM = 128
QK_DIM = 64
ROT_DIM = 16
ROPE_THETA = 500000.0
N_RNN_HEADS = 4
RNN_DIM = 128
ATT_WIDTH = N_ATT_HEADS * V_DIM
RNN_WIDTH = N_RNN_HEADS * RNN_DIM
IN_WIDTH = 3 * ATT_WIDTH + 4 * RNN_WIDTH
N_EXPERTS = 8
EPS = 1e-6
NEG = -1e30
LANES = 128
V7X_VMEM_BYTES = 64 * 1024 * 1024
HGRN_CHUNK = 128
HGRN_SUB = 16

NT_DIMS = (((1,), (1,)), ((), ()))


def _params(semantics, vmem_mb):
    return pltpu.CompilerParams(dimension_semantics=semantics,
                                vmem_limit_bytes=vmem_mb * 1024 * 1024)


def _sigmoid(x):
    return 1.0 / (1.0 + jnp.exp(-x))


def _mm(a, b, exact, dims=None):
    if exact:
        a, b, prec = a.astype(F32), b.astype(F32), lax.Precision.HIGHEST
    else:
        a, b, prec = a.astype(BF16), b.astype(BF16), None
    if dims is None:
        return jnp.dot(a, b, precision=prec, preferred_element_type=F32)
    return lax.dot_general(a, b, dims, precision=prec, preferred_element_type=F32)


def _split_bf16(x, parts):
    out = []
    for _ in range(parts - 1):
        h = x.astype(BF16)
        out.append(h)
        x = x - h.astype(F32)
    out.append(x.astype(BF16))
    return out


def _group_mean_sq(x, bd):
    parts = _split_bf16(x * x, 2)
    return (jnp.dot(parts[0], bd, preferred_element_type=F32)
            + jnp.dot(parts[1], bd, preferred_element_type=F32))


def _qk_norm_rope(z, gain, bd, cos, sin_hi, sin_lo):
    r = lax.rsqrt(_group_mean_sq(z, bd) + EPS)
    y = z * r * gain
    half = ROT_DIM // 2
    up = pltpu.roll(y, LANES - half, axis=1)
    dn = pltpu.roll(y, half, axis=1)
    return y * cos + up * sin_hi + dn * sin_lo


def _inproj_kernel(x_ref, g_ref, w_ref, qg_ref, kg_ref, bd_ref, cos_ref, shi_ref, slo_ref,
                   q_ref, k_ref, v_ref, r_ref, kb_ref, vb_ref, *, exact):
    x = x_ref[...]
    r = lax.rsqrt(jnp.mean(x * x, axis=-1, keepdims=True) + EPS)
    hn = x * r * g_ref[...]
    if not exact:
        hn = hn.astype(BF16)
    bd = bd_ref[...]
    cos, shi, slo = cos_ref[...], shi_ref[...], slo_ref[...]
    for h in range(N_ATT_HEADS):
        lo, hi = h * LANES, (h + 1) * LANES
        zq = _mm(hn, w_ref[:, lo:hi], exact)
        q_ref[:, lo:hi] = _qk_norm_rope(zq, qg_ref[...], bd, cos, shi, slo)
        zk = _mm(hn, w_ref[:, ATT_WIDTH + lo:ATT_WIDTH + hi], exact)
        k_head = _qk_norm_rope(zk, kg_ref[...], bd, cos, shi, slo)
        k_ref[:, lo:hi] = k_head
        kb_ref[:, lo:hi] = k_head.astype(BF16)
    v = _mm(hn, w_ref[:, 2 * ATT_WIDTH:3 * ATT_WIDTH], exact)
    v_ref[...] = v
    vb_ref[...] = v.astype(BF16)
    for j in range(4):
        lo, hi = j * RNN_WIDTH, (j + 1) * RNN_WIDTH
        r_ref[:, lo:hi] = _mm(hn, w_ref[:, 3 * ATT_WIDTH + lo:3 * ATT_WIDTH + hi], exact)


def _input_projection(x, ln_g, w, qg, kg, bd, tabs, tm):
    n = x.shape[0]
    cos, shi, slo = tabs
    nt = cos.shape[0] // tm
    row = lambda i: (i, 0)
    fixed = lambda i: (0, 0)
    tab = lambda i: (i % nt, 0)
    return pl.pallas_call(
        functools.partial(_inproj_kernel, exact=w.dtype == F32),
        grid=(n // tm,),
        in_specs=[
            pl.BlockSpec((tm, D_MODEL), row),
            pl.BlockSpec((1, D_MODEL), fixed),
            pl.BlockSpec((D_MODEL, IN_WIDTH), fixed),
            pl.BlockSpec((1, LANES), fixed),
            pl.BlockSpec((1, LANES), fixed),
            pl.BlockSpec((LANES, LANES), fixed),
            pl.BlockSpec((tm, LANES), tab),
            pl.BlockSpec((tm, LANES), tab),
            pl.BlockSpec((tm, LANES), tab),
        ],
        out_specs=[
            pl.BlockSpec((tm, ATT_WIDTH), row),
            pl.BlockSpec((tm, ATT_WIDTH), row),
            pl.BlockSpec((tm, ATT_WIDTH), row),
            pl.BlockSpec((tm, 4 * RNN_WIDTH), row),
            pl.BlockSpec((tm, ATT_WIDTH), row),
            pl.BlockSpec((tm, ATT_WIDTH), row),
        ],
        out_shape=[
            jax.ShapeDtypeStruct((n, ATT_WIDTH), F32),
            jax.ShapeDtypeStruct((n, ATT_WIDTH), F32),
            jax.ShapeDtypeStruct((n, ATT_WIDTH), F32),
            jax.ShapeDtypeStruct((n, 4 * RNN_WIDTH), F32),
            jax.ShapeDtypeStruct((n, ATT_WIDTH), BF16),
            jax.ShapeDtypeStruct((n, ATT_WIDTH), BF16),
        ],
        compiler_params=_params(("parallel",), 48),
        name="input_projection",
    )(x, ln_g, w, qg, kg, bd, cos, shi, slo)


def _rope_tables(start, t):
    half = ROT_DIM // 2
    inv = np.power(ROPE_THETA, -np.arange(half, dtype=np.float64) * 2.0 / ROT_DIM)
    pos = start + np.arange(t)
    coarse = np.arange(pos[0] // LANES, pos[-1] // LANES + 1, dtype=np.float64)[:, None] * LANES * inv
    fine = np.arange(LANES, dtype=np.float64)[:, None] * inv
    const = lambda a: jnp.asarray(a.astype(np.float32))
    a_idx = jnp.asarray(pos // LANES - pos[0] // LANES)
    b_idx = jnp.asarray(pos % LANES)
    ca, sa = const(np.cos(coarse))[a_idx], const(np.sin(coarse))[a_idx]
    cb, sb = const(np.cos(fine))[b_idx], const(np.sin(fine))[b_idx]
    cos, sin = ca * cb - sa * sb, sa * cb + ca * sb
    one = jnp.ones((t, QK_DIM - ROT_DIM), F32)
    zero = jnp.zeros((t, QK_DIM - ROT_DIM), F32)
    zh = jnp.zeros((t, half), F32)
    comp = lambda a, b, rest: jnp.concatenate([a, b, rest], axis=1)
    c64 = comp(cos, cos, one)
    hi64 = comp(-sin, zh, zero)
    lo64 = comp(zh, sin, zero)
    two = lambda a: jnp.concatenate([a, a], axis=1)
    return two(c64), two(hi64), two(lo64)


ATT_HEADS_PER_STEP = 2


def _attn_prompt_kernel(lam_ref, q_ref, k_ref, v_ref, g_ref, o_ref, acc, *, tq, lam_scale):
    tk = tq
    qi = pl.program_id(2)
    heads = range(ATT_HEADS_PER_STEP)
    lane = lax.broadcasted_iota(jnp.int32, (tq, V_DIM), 1)
    q_cat = []
    for h in heads:
        q = q_ref[:, h * V_DIM:(h + 1) * V_DIM] * (QK_DIM ** -0.5)
        q_cat.append(jnp.concatenate([jnp.where(lane < QK_DIM, q, 0.0), jnp.where(lane >= QK_DIM, q, 0.0)],
                                     axis=0).astype(BF16))
    acc[...] = jnp.zeros_like(acc)

    def tile(j, carry, diagonal):
        start = pl.multiple_of(j * tk, tk)
        new = []
        head_cols = [slice(h * V_DIM, (h + 1) * V_DIM) for h in heads]
        scores = [lax.dot_general(q_cat[h], k_ref[pl.ds(start, tk), head_cols[h]], NT_DIMS,
                                  preferred_element_type=F32) for h in heads]
        for h in heads:
            cols_h = head_cols[h]
            s = scores[h]
            if diagonal:
                rows = lax.broadcasted_iota(jnp.int32, (2 * tq, tk), 0) % tq
                cols = lax.broadcasted_iota(jnp.int32, (2 * tq, tk), 1)
                s = jnp.where(cols <= rows, s, NEG)
            m_new = jnp.maximum(carry[h], jnp.max(s, axis=-1, keepdims=True))
            alpha = jnp.exp(carry[h] - m_new)
            p = jnp.exp(s - m_new).astype(BF16)
            v_ones = jnp.concatenate([v_ref[pl.ds(start, tk), cols_h], jnp.ones((tk, V_DIM), BF16)], axis=1)
            acc[h] = alpha * acc[h] + jnp.dot(p, v_ones, preferred_element_type=F32)
            new.append(m_new)
        return tuple(new)

    carry = tuple(jnp.full((2 * tq, 1), NEG, F32) for _ in heads)
    carry = lax.fori_loop(0, qi, lambda j, c: tile(j, c, False), carry)
    tile(qi, carry, True)
    for h in heads:
        o2 = acc[h, :, :V_DIM] / acc[h, :, V_DIM:]
        o = o2[:tq] - lam_ref[0] * o2[tq:]
        r = lax.rsqrt(jnp.mean(o * o, axis=-1, keepdims=True) + EPS)
        o_ref[:, h * V_DIM:(h + 1) * V_DIM] = o * r * g_ref[...] * lam_scale


def _attention_prompt(q, k, v, lam, subln_g, lam_init, batch, seq, tq):
    nq = seq // tq
    hp = ATT_HEADS_PER_STEP
    width = hp * V_DIM
    kernel = functools.partial(_attn_prompt_kernel, tq=tq, lam_scale=1.0 - lam_init)
    return pl.pallas_call(
        kernel,
        grid=(batch, N_ATT_HEADS // hp, nq),
        in_specs=[
            pl.BlockSpec(memory_space=pltpu.SMEM),
            pl.BlockSpec((tq, width), lambda b, h, i: (b * nq + i, h)),
            pl.BlockSpec((seq, width), lambda b, h, i: (b, h)),
            pl.BlockSpec((seq, width), lambda b, h, i: (b, h)),
            pl.BlockSpec((1, V_DIM), lambda b, h, i: (0, 0)),
        ],
        out_specs=pl.BlockSpec((tq, width), lambda b, h, i: (b * nq + i, h)),
        out_shape=jax.ShapeDtypeStruct((batch * seq, ATT_WIDTH), F32),
        scratch_shapes=[pltpu.VMEM((hp, 2 * tq, 2 * V_DIM), F32)],
        compiler_params=_params(("parallel", "parallel", "arbitrary"), 40),
        name="attention_prompt",
    )(lam, q, k, v, subln_g)


def _two_term(x):
    return jnp.concatenate(_split_bf16(x, 2), axis=0)


def _attn_decode_kernel(pt_ref, lam_ref, q_ref, kn_ref, vn_ref, *rest, n_pages, lam_scale):
    k_refs, v_refs = rest[:n_pages], rest[n_pages:2 * n_pages]
    g_ref, o_ref = rest[2 * n_pages:]
    rows_per_page = k_refs[0].shape[0]
    n_rows = 2 * N_ATT_HEADS
    lane = lax.broadcasted_iota(jnp.int32, (N_ATT_HEADS, V_DIM), 1)
    q4 = q_ref[0] * (QK_DIM ** -0.5)
    qm = jnp.concatenate([jnp.where(lane < QK_DIM, q4, 0.0), jnp.where(lane >= QK_DIM, q4, 0.0)], axis=0)
    q_cat = _two_term(qm)

    def two_term_dot(lhs_cat, rhs, dims):
        out = sum(lax.dot_general(lhs_cat, part, dims, preferred_element_type=F32)
                  for part in _split_bf16(rhs, 2))
        return out[:n_rows] + out[n_rows:]

    r_i = lax.broadcasted_iota(jnp.int32, (n_rows, rows_per_page), 0)
    c_i = lax.broadcasted_iota(jnp.int32, (n_rows, rows_per_page), 1)
    keep = (c_i % N_ATT_HEADS) == (r_i % N_ATT_HEADS)
    scores = [jnp.where(keep, two_term_dot(q_cat, k_ref[...], NT_DIMS), NEG) for k_ref in k_refs]

    prod = q4 * kn_ref[0]
    s_new = jnp.concatenate([jnp.sum(jnp.where(lane < QK_DIM, prod, 0.0), axis=-1, keepdims=True),
                             jnp.sum(jnp.where(lane >= QK_DIM, prod, 0.0), axis=-1, keepdims=True)], axis=0)
    m = s_new
    for s in scores:
        m = jnp.maximum(m, jnp.max(s, axis=-1, keepdims=True))
    p_new = jnp.exp(s_new - m)
    vn = vn_ref[0]
    l = p_new
    acc = p_new * jnp.concatenate([vn, vn], axis=0)
    for s, v_ref in zip(scores, v_refs):
        p = jnp.exp(s - m)
        l = l + jnp.sum(p, axis=-1, keepdims=True)
        acc = acc + two_term_dot(_two_term(p), v_ref[...], (((1,), (0,)), ((), ())))
    o8 = acc / l
    o = o8[:N_ATT_HEADS] - lam_ref[0] * o8[N_ATT_HEADS:]
    r = lax.rsqrt(jnp.mean(o * o, axis=-1, keepdims=True) + EPS)
    o_ref[0] = o * r * g_ref[...] * lam_scale


def _attention_decode(q, k_new, v_new, cache_k, cache_v, page_table, layer, lam, subln_g, lam_init):
    nb = q.shape[0]
    n_pages = page_table.shape[1]
    rows = cache_k.shape[2]
    heads3 = lambda a: a.reshape(nb, N_ATT_HEADS, V_DIM)
    tok = lambda b, pt: (b, 0, 0)
    page_specs = [pl.BlockSpec((None, None, rows, V_DIM),
                               functools.partial(lambda p, b, pt: (layer, pt[b * n_pages + p], 0, 0), p))
                  for p in range(n_pages)]
    kernel = functools.partial(_attn_decode_kernel, n_pages=n_pages, lam_scale=1.0 - lam_init)
    out = pl.pallas_call(
        kernel,
        grid_spec=pltpu.PrefetchScalarGridSpec(
            num_scalar_prefetch=1,
            grid=(nb,),
            in_specs=[
                pl.BlockSpec(memory_space=pltpu.SMEM),
                pl.BlockSpec((1, N_ATT_HEADS, V_DIM), tok),
                pl.BlockSpec((1, N_ATT_HEADS, V_DIM), tok),
                pl.BlockSpec((1, N_ATT_HEADS, V_DIM), tok),
                *page_specs, *page_specs,
                pl.BlockSpec((1, V_DIM), lambda b, pt: (0, 0)),
            ],
            out_specs=pl.BlockSpec((1, N_ATT_HEADS, V_DIM), tok),
        ),
        out_shape=jax.ShapeDtypeStruct((nb, N_ATT_HEADS, V_DIM), F32),
        compiler_params=_params(("parallel",), 40),
        name="attention_decode",
    )(page_table.reshape(-1), lam, heads3(q), heads3(k_new), heads3(v_new),
      *([cache_k] * n_pages), *([cache_v] * n_pages), subln_g)
    return out.reshape(nb, ATT_WIDTH)


def _hgrn_gates(fx, log_lb, log1m_lb, one_m_lb):
    e = jnp.exp(-jnp.abs(fx))
    log_sig = jnp.minimum(fx, 0.0) - jnp.log(1.0 + e)
    b = log1m_lb + log_sig
    log_f = jnp.maximum(log_lb, b) + jnp.log(1.0 + jnp.exp(-jnp.abs(log_lb - b)))
    k = one_m_lb * (jnp.where(fx >= 0.0, e, 1.0) / (1.0 + e))
    return log_f, k


def _rnn_out(o, gate, norm_g):
    r = lax.rsqrt(jnp.mean(o * o, axis=-1, keepdims=True) + EPS)
    return o * r * norm_g * (gate * _sigmoid(gate))


def _hgrn_prompt_kernel(q_ref, f_ref, i_ref, g_ref, lb_ref, ng_ref, tri_ref, o_ref, s_ref, s_scr):
    c = pl.program_id(1)
    n_chunks = pl.num_programs(1)
    ch, sub = HGRN_CHUNK, HGRN_SUB
    n_sub = ch // sub

    @pl.when(c == 0)
    def _():
        s_scr[...] = jnp.zeros_like(s_scr)

    tri = tri_ref[...]
    row_c = lax.broadcasted_iota(jnp.int32, (ch, RNN_DIM), 0)
    row_s = lax.broadcasted_iota(jnp.int32, (sub, RNN_DIM), 0)
    lane_s = lax.broadcasted_iota(jnp.int32, (sub, ch), 1)
    eye = (lax.broadcasted_iota(jnp.int32, (RNN_DIM, RNN_DIM), 0)
           == lax.broadcasted_iota(jnp.int32, (RNN_DIM, RNN_DIM), 1))

    for h in range(N_RNN_HEADS):
        lo, hi = h * RNN_DIM, (h + 1) * RNN_DIM
        log_f, k = _hgrn_gates(f_ref[:, lo:hi], lb_ref[0:1, lo:hi], lb_ref[1:2, lo:hi], lb_ref[2:3, lo:hi])
        qx = q_ref[:, lo:hi]
        q = qx * _sigmoid(qx)
        v_bf = i_ref[:, lo:hi].astype(BF16)
        b = sum(jnp.dot(tri, part, preferred_element_type=F32) for part in _split_bf16(log_f, 3))
        state = s_scr[h]
        o = jnp.dot((q * jnp.exp(b)).astype(BF16), state.astype(BF16), preferred_element_type=F32)

        a_rows = []
        for blk in range(n_sub):
            r0 = blk * sub
            q_b, b_b, k_b = q[r0:r0 + sub], b[r0:r0 + sub], k[r0:r0 + sub]
            if blk == 0:
                a_blk = jnp.zeros((sub, ch), F32)
            else:
                ref_row = b[r0 - 1:r0]
                q_dec = (q_b * jnp.exp(b_b - ref_row)).astype(BF16)
                k_dec = jnp.where(row_c < r0, k * jnp.exp(jnp.minimum(ref_row - b, 0.0)), 0.0).astype(BF16)
                a_blk = lax.dot_general(q_dec, k_dec, NT_DIMS, preferred_element_type=F32)
            for s in range(sub):
                w = q_b * jnp.exp(b_b - b_b[s:s + 1]) * k_b[s:s + 1]
                col = jnp.sum(jnp.where(row_s >= s, w, 0.0), axis=-1, keepdims=True)
                a_blk = jnp.where(lane_s == r0 + s, col, a_blk)
            a_rows.append(a_blk)
        a = jnp.concatenate(a_rows, axis=0)
        o = o + jnp.dot(a.astype(BF16), v_bf, preferred_element_type=F32)

        b_last = b[ch - 1:ch]
        decay_col = jnp.sum(jnp.where(eye, jnp.exp(b_last), 0.0), axis=1, keepdims=True)
        k_end = (k * jnp.exp(b_last - b)).T.astype(BF16)
        new_state = state * decay_col + jnp.dot(k_end, v_bf, preferred_element_type=F32)
        s_scr[h] = new_state
        o_ref[:, lo:hi] = _rnn_out(o, g_ref[:, lo:hi], ng_ref[...])

        @pl.when(c == n_chunks - 1)
        def _():
            s_ref[0, h] = new_state


def _hgrn_prompt(r, lb_rows, norm_g, tri, batch, seq):
    ch = HGRN_CHUNK
    nc = seq // ch
    col = lambda j: pl.BlockSpec((ch, RNN_WIDTH), lambda b, c: (b * nc + c, j))
    fixed = lambda b, c: (0, 0)
    return pl.pallas_call(
        _hgrn_prompt_kernel,
        grid=(batch, nc),
        in_specs=[col(0), col(1), col(2), col(3),
                  pl.BlockSpec((3, RNN_WIDTH), fixed),
                  pl.BlockSpec((1, RNN_DIM), fixed),
                  pl.BlockSpec((ch, ch), fixed)],
        out_specs=[
            pl.BlockSpec((ch, RNN_WIDTH), lambda b, c: (b * nc + c, 0)),
            pl.BlockSpec((1, N_RNN_HEADS, RNN_DIM, RNN_DIM), lambda b, c: (b, 0, 0, 0)),
        ],
        out_shape=[
            jax.ShapeDtypeStruct((batch * seq, RNN_WIDTH), F32),
            jax.ShapeDtypeStruct((batch, N_RNN_HEADS, RNN_DIM, RNN_DIM), F32),
        ],
        scratch_shapes=[pltpu.VMEM((N_RNN_HEADS, RNN_DIM, RNN_DIM), F32)],
        compiler_params=_params(("parallel", "arbitrary"), 32),
        name="hgrn_prompt",
    )(r, r, r, r, lb_rows, norm_g, tri)


def _hgrn_decode_kernel(r_ref, s_ref, lb_ref, ng_ref, o_ref, so_ref):
    eye = (lax.broadcasted_iota(jnp.int32, (RNN_DIM, RNN_DIM), 0)
           == lax.broadcasted_iota(jnp.int32, (RNN_DIM, RNN_DIM), 1))
    to_col = lambda row: jnp.sum(jnp.where(eye, row, 0.0), axis=1, keepdims=True)
    for h in range(N_RNN_HEADS):
        seg = lambda j: r_ref[0, :, j * RNN_WIDTH + h * RNN_DIM:j * RNN_WIDTH + (h + 1) * RNN_DIM]
        lo, hi = h * RNN_DIM, (h + 1) * RNN_DIM
        log_f, k = _hgrn_gates(seg(1), lb_ref[0:1, lo:hi], lb_ref[1:2, lo:hi], lb_ref[2:3, lo:hi])
        qx = seg(0)
        q = qx * _sigmoid(qx)
        new_state = s_ref[0, h] * to_col(jnp.exp(log_f)) + to_col(k) * seg(2)
        so_ref[0, h] = new_state
        o = jnp.sum(to_col(q) * new_state, axis=0, keepdims=True)
        o_ref[0, :, lo:hi] = _rnn_out(o, seg(3), ng_ref[...])


def _hgrn_decode(r, state, layer, lb_rows, norm_g):
    nb = r.shape[0]
    out, new_state = pl.pallas_call(
        _hgrn_decode_kernel,
        grid=(nb,),
        in_specs=[
            pl.BlockSpec((1, 1, 4 * RNN_WIDTH), lambda b: (b, 0, 0)),
            pl.BlockSpec((None, 1, N_RNN_HEADS, RNN_DIM, RNN_DIM), lambda b: (layer, b, 0, 0, 0)),
            pl.BlockSpec((3, RNN_WIDTH), lambda b: (0, 0)),
            pl.BlockSpec((1, RNN_DIM), lambda b: (0, 0)),
        ],
        out_specs=[
            pl.BlockSpec((1, 1, RNN_WIDTH), lambda b: (b, 0, 0)),
            pl.BlockSpec((1, N_RNN_HEADS, RNN_DIM, RNN_DIM), lambda b: (b, 0, 0, 0)),
        ],
        out_shape=[
            jax.ShapeDtypeStruct((nb, 1, RNN_WIDTH), F32),
            jax.ShapeDtypeStruct((nb, N_RNN_HEADS, RNN_DIM, RNN_DIM), F32),
        ],
        compiler_params=_params(("parallel",), 32),
        name="hgrn_decode",
    )(r.reshape(nb, 1, 4 * RNN_WIDTH), state, lb_rows, norm_g)
    return out.reshape(nb, RNN_WIDTH), new_state


def _outproj_kernel(x_ref, att_ref, rnn_ref, w_ref, g_ref, *rest, routed, exact):
    if routed:
        wr_ref, xo_ref, h_ref, comb_ref = rest
    else:
        xo_ref, h_ref = rest
    mix = (_mm(att_ref[...], w_ref[:ATT_WIDTH, :], exact)
           + _mm(rnn_ref[...], w_ref[ATT_WIDTH:, :], exact))
    x = x_ref[...] + mix
    xo_ref[...] = x
    r = lax.rsqrt(jnp.mean(x * x, axis=-1, keepdims=True) + EPS)
    hf = x * r * g_ref[...]
    h_ref[...] = hf.astype(h_ref.dtype)
    if routed:
        logits = _mm(hf, wr_ref[...], True)
        lane = lax.broadcasted_iota(jnp.int32, logits.shape, 1)
        lane_f = lane.astype(F32)
        logits = jnp.where(lane < N_EXPERTS, logits, -jnp.inf)
        v1 = jnp.max(logits, axis=-1, keepdims=True)
        i1 = jnp.min(jnp.where(logits == v1, lane_f, float(LANES)), axis=-1, keepdims=True)
        first = lane_f == i1
        rest_logits = jnp.where(first, -jnp.inf, logits)
        v2 = jnp.max(rest_logits, axis=-1, keepdims=True)
        i2 = jnp.min(jnp.where(rest_logits == v2, lane_f, float(LANES)), axis=-1, keepdims=True)
        second = lane_f == i2
        e = jnp.exp(v2 - v1)
        g1 = 1.0 / (1.0 + e)
        picked = (lane_f - N_EXPERTS == i1) | (lane_f - N_EXPERTS == i2)
        comb_ref[...] = (jnp.where(first, g1, 0.0) + jnp.where(second, e * g1, 0.0)
                         + jnp.where(picked, 1.0, 0.0))


def _output_projection(x, att, rnn, w, ln_g, router, tm):
    exact = w.dtype == F32
    n = x.shape[0]
    row = lambda i: (i, 0)
    fixed = lambda i: (0, 0)
    in_specs = [
        pl.BlockSpec((tm, D_MODEL), row),
        pl.BlockSpec((tm, ATT_WIDTH), row),
        pl.BlockSpec((tm, RNN_WIDTH), row),
        pl.BlockSpec((D_MODEL, D_MODEL), fixed),
        pl.BlockSpec((1, D_MODEL), fixed),
    ]
    out_specs = [pl.BlockSpec((tm, D_MODEL), row), pl.BlockSpec((tm, D_MODEL), row)]
    out_shape = [jax.ShapeDtypeStruct((n, D_MODEL), F32), jax.ShapeDtypeStruct((n, D_MODEL), F32 if exact else BF16)]
    args = [x, att, rnn, w, ln_g]
    if router is not None:
        in_specs.append(pl.BlockSpec((D_MODEL, LANES), fixed))
        out_specs.append(pl.BlockSpec((tm, LANES), row))
        out_shape.append(jax.ShapeDtypeStruct((n, LANES), F32))
        args.append(router)
    return pl.pallas_call(
        functools.partial(_outproj_kernel, routed=router is not None, exact=exact),
        grid=(n // tm,),
        in_specs=in_specs,
        out_specs=out_specs,
        out_shape=out_shape,
        compiler_params=_params(("parallel",), 32),
        name="output_projection",
    )(*args)


def _ffn_kernel(x_ref, h_ref, wg_ref, wu_ref, wd_ref, y_ref, acc, *, exact):
    f = pl.program_id(1)

    @pl.when(f == 0)
    def _():
        acc[...] = jnp.zeros_like(acc)

    h = h_ref[...]
    g = _mm(h, wg_ref[...], exact)
    u = _mm(h, wu_ref[...], exact)
    acc[...] += _mm(g * _sigmoid(g) * u, wd_ref[...], exact)

    @pl.when(f == pl.num_programs(1) - 1)
    def _():
        y_ref[...] = x_ref[...] + acc[...]


def _dense_ffn(x, h, w_gu, w_dn, tm, tf):
    n = x.shape[0]
    d_ff = w_dn.shape[0]
    nf = d_ff // tf
    return pl.pallas_call(
        functools.partial(_ffn_kernel, exact=w_gu.dtype == F32),
        grid=(n // tm, nf),
        in_specs=[
            pl.BlockSpec((tm, D_MODEL), lambda i, f: (i, 0)),
            pl.BlockSpec((tm, D_MODEL), lambda i, f: (i, 0)),
            pl.BlockSpec((D_MODEL, tf), lambda i, f: (0, f)),
            pl.BlockSpec((D_MODEL, tf), lambda i, f: (0, f + nf)),
            pl.BlockSpec((tf, D_MODEL), lambda i, f: (f, 0)),
        ],
        out_specs=pl.BlockSpec((tm, D_MODEL), lambda i, f: (i, 0)),
        out_shape=jax.ShapeDtypeStruct((n, D_MODEL), F32),
        scratch_shapes=[pltpu.VMEM((tm, D_MODEL), F32)],
        compiler_params=_params(("parallel", "arbitrary"), 48),
        name="dense_ffn",
    )(x, h, w_gu, w_gu, w_dn)


def _moe_kernel(x_ref, h_ref, comb_ref, wg_ref, wu_ref, wd_ref, y_ref, acc):
    e = pl.program_id(1)
    f = pl.program_id(2)

    @pl.when((e == 0) & (f == 0))
    def _():
        acc[...] = jnp.zeros_like(acc)

    comb = comb_ref[...]
    lane = lax.broadcasted_iota(jnp.int32, comb.shape, 1)
    gate = jnp.sum(jnp.where(lane == e, comb, 0.0), axis=-1, keepdims=True)
    h = h_ref[...]
    g = jnp.dot(h, wg_ref[...], preferred_element_type=F32)
    u = jnp.dot(h, wu_ref[...], preferred_element_type=F32)
    a = (g * _sigmoid(g) * u).astype(BF16)
    acc[...] += gate * jnp.dot(a, wd_ref[...], preferred_element_type=F32)

    @pl.when((e == pl.num_programs(1) - 1) & (f == pl.num_programs(2) - 1))
    def _():
        y_ref[...] = x_ref[...] + acc[...]


def _moe_ffn(x, h, comb, w_gu, w_dn, tm, tf):
    n = x.shape[0]
    d_ff = w_dn.shape[1]
    nf = d_ff // tf
    return pl.pallas_call(
        _moe_kernel,
        grid=(n // tm, N_EXPERTS, nf),
        in_specs=[
            pl.BlockSpec((tm, D_MODEL), lambda i, e, f: (i, 0)),
            pl.BlockSpec((tm, D_MODEL), lambda i, e, f: (i, 0)),
            pl.BlockSpec((tm, LANES), lambda i, e, f: (i, 0)),
            pl.BlockSpec((None, D_MODEL, tf), lambda i, e, f: (e, 0, f)),
            pl.BlockSpec((None, D_MODEL, tf), lambda i, e, f: (e, 0, f + nf)),
            pl.BlockSpec((None, tf, D_MODEL), lambda i, e, f: (e, f, 0)),
        ],
        out_specs=pl.BlockSpec((tm, D_MODEL), lambda i, e, f: (i, 0)),
        out_shape=jax.ShapeDtypeStruct((n, D_MODEL), F32),
        scratch_shapes=[pltpu.VMEM((tm, D_MODEL), F32)],
        compiler_params=_params(("parallel", "arbitrary", "arbitrary"), 48),
        name="moe_ffn",
    )(x, h, comb, w_gu, w_gu, w_dn)


MOE_ROWS = 512
MOE_TOKS = 512
MOE_WIN = 256


def _route(comb, n_tok):
    rb, tb = MOE_ROWS, MOE_TOKS
    i32 = jnp.int32
    nb_max = 2 * n_tok // rb + N_EXPERTS
    n_sub = n_tok // tb
    sel = comb[:, N_EXPERTS:2 * N_EXPERTS] > 0.5
    csum = jnp.cumsum(sel.astype(i32), axis=0)
    cnt = csum[-1]
    nblk = (cnt + rb - 1) // rb
    blk_end = jnp.cumsum(nblk)
    blk_off = blk_end - nblk
    dest = jnp.where(sel, blk_off[None, :] * rb + csum - 1, -1)

    blk = jnp.arange(nb_max, dtype=i32)
    count_le = lambda ends, x: jnp.sum((ends[None, :] <= x[:, None]).astype(i32), axis=1)
    blk_e = jnp.minimum(count_le(blk_end, blk), N_EXPERTS - 1)
    blk_ok = blk < blk_end[-1]
    k = blk - blk_off[blk_e]
    csum_blk = csum.T[blk_e]
    first_with = lambda _, r: jnp.sum((csum_blk < r[:, None]).astype(i32), axis=1)
    j_lo = jnp.where(blk_ok, first_with(blk_e, k * rb + 1) // tb, 0).astype(i32)
    j_hi = jnp.where(blk_ok, first_with(blk_e, jnp.minimum((k + 1) * rb, cnt[blk_e])) // tb, 0).astype(i32)
    n_pair = j_hi - j_lo + 1
    pair_end = jnp.cumsum(n_pair)
    pair_off = pair_end - n_pair
    n_pairs_max = nb_max + N_EXPERTS * n_sub
    i = jnp.arange(n_pairs_max, dtype=i32)
    live = i < pair_end[-1]
    p_blk = jnp.minimum(count_le(pair_end, i), nb_max - 1)
    p_j = jnp.where(live, j_lo[p_blk] + i - pair_off[p_blk], j_hi[nb_max - 1]).astype(i32)
    p_flags = ((live & (i == pair_off[p_blk])).astype(i32)
               + 2 * (live & blk_ok[p_blk]).astype(i32))
    gather_tabs = (p_blk, p_j, p_flags, blk_e[p_blk])

    win = MOE_WIN
    ends = csum[tb - 1::tb]
    starts = jnp.concatenate([jnp.zeros((1, N_EXPERTS), i32), ends[:-1]], axis=0)
    has = (ends > starts).reshape(-1)
    c_lo = ((blk_off[None, :] * rb + starts) // win).reshape(-1)
    c_hi = ((blk_off[None, :] * rb + ends - 1) // win).reshape(-1)
    n_win = jnp.where(has, c_hi - c_lo + 1, 0)
    win_end = jnp.cumsum(n_win)
    win_off = win_end - n_win
    n_win_max = nb_max * (rb // win) + N_EXPERTS * n_sub
    w = jnp.arange(n_win_max, dtype=i32)
    w_live = w < win_end[-1]
    w_pair = jnp.minimum(count_le(win_end, w), n_sub * N_EXPERTS - 1)
    w_blk = jnp.where(w_live, c_lo[w_pair] + w - win_off[w_pair], 0).astype(i32)
    w_tok = w_pair // N_EXPERTS
    tok_first = jnp.concatenate([jnp.zeros((1,), i32), win_end[N_EXPERTS - 1::N_EXPERTS]])
    w_flags = ((w_live & (w == tok_first[w_tok])).astype(i32)
               + 2 * (w_live & (w == tok_first[w_tok + 1] - 1)).astype(i32)
               + 4 * w_live.astype(i32))
    combine_tabs = (w_tok.astype(i32), (w_pair % N_EXPERTS).astype(i32), w_blk, w_flags)
    return dest, gather_tabs, (blk_e, blk_ok.astype(i32)), combine_tabs, nb_max


def _moe_gather_kernel(blk_ref, j_ref, flag_ref, e_ref, h_ref, dest_ref, o_ref):
    i = pl.program_id(0)
    flags = flag_ref[i]

    @pl.when(flags % 2 == 1)
    def _():
        o_ref[...] = jnp.zeros_like(o_ref)

    @pl.when(flags >= 2)
    def _():
        d = dest_ref[...]
        sub = lax.broadcasted_iota(jnp.int32, d.shape, 0)
        d_e = jnp.sum(jnp.where(sub == e_ref[i], d, 0.0), axis=0, keepdims=True)
        rows = lax.broadcasted_iota(jnp.int32, (MOE_ROWS, MOE_TOKS), 0).astype(F32)
        base = (blk_ref[i] * MOE_ROWS).astype(F32)
        onehot = jnp.where(d_e - base == rows, 1.0, 0.0).astype(BF16)
        o_ref[...] = (o_ref[...].astype(F32)
                      + jnp.dot(onehot, h_ref[...], preferred_element_type=F32)).astype(BF16)


def _moe_group_kernel(e_ref, ok_ref, h_ref, wg_ref, wu_ref, wd_ref, o_ref, acc):
    b = pl.program_id(0)
    f = pl.program_id(1)

    @pl.when(f == 0)
    def _():
        acc[...] = jnp.zeros_like(acc)

    @pl.when(ok_ref[b] == 1)
    def _():
        h = h_ref[...]
        g = jnp.dot(h, wg_ref[...], preferred_element_type=F32)
        u = jnp.dot(h, wu_ref[...], preferred_element_type=F32)
        acc[...] += jnp.dot((g * _sigmoid(g) * u).astype(BF16), wd_ref[...], preferred_element_type=F32)

    @pl.when(f == pl.num_programs(1) - 1)
    def _():
        o_ref[...] = acc[...].astype(BF16)


def _moe_combine_kernel(tok_ref, e_ref, blk_ref, flag_ref, x_ref, comb_ref, dest_ref, rows_ref, y_ref, acc):
    i = pl.program_id(0)
    flags = flag_ref[i]

    @pl.when(flags % 2 == 1)
    def _():
        acc[...] = jnp.zeros_like(acc)

    @pl.when(flags >= 4)
    def _():
        lane = lax.broadcasted_iota(jnp.int32, comb_ref.shape, 1)
        mine = lane == e_ref[i]
        d = jnp.sum(jnp.where(mine, dest_ref[...], 0.0), axis=-1, keepdims=True)
        gate = jnp.sum(jnp.where(mine, comb_ref[...], 0.0), axis=-1, keepdims=True)
        cols = lax.broadcasted_iota(jnp.int32, (MOE_TOKS, MOE_WIN), 1).astype(F32)
        base = (blk_ref[i] * MOE_WIN).astype(F32)
        onehot = jnp.where(d - base == cols, 1.0, 0.0).astype(BF16)
        acc[...] += gate * jnp.dot(onehot, rows_ref[...], preferred_element_type=F32)

    @pl.when((flags // 2) % 2 == 1)
    def _():
        y_ref[...] = x_ref[...] + acc[...]


def _moe_routed(x, h, comb, w_gu, w_dn, tf):
    n = x.shape[0]
    rb, tb = MOE_ROWS, MOE_TOKS
    d_ff = w_dn.shape[1]
    nf = d_ff // tf
    dest, gather_tabs, (blk_e, blk_ok), combine_tabs, nb_max = _route(comb, n)
    dest_f = dest.astype(F32)
    dest_rows = dest_f.T
    dest_lanes = jnp.pad(dest_f, ((0, 0), (0, LANES - N_EXPERTS)), constant_values=-1.0)

    n_pairs = gather_tabs[0].shape[0]
    h_sorted = pl.pallas_call(
        _moe_gather_kernel,
        grid_spec=pltpu.PrefetchScalarGridSpec(
            num_scalar_prefetch=4,
            grid=(n_pairs,),
            in_specs=[
                pl.BlockSpec((tb, D_MODEL), lambda i, blk, j, fl, e: (j[i], 0)),
                pl.BlockSpec((N_EXPERTS, tb), lambda i, blk, j, fl, e: (0, j[i])),
            ],
            out_specs=pl.BlockSpec((rb, D_MODEL), lambda i, blk, j, fl, e: (blk[i], 0)),
        ),
        out_shape=jax.ShapeDtypeStruct((nb_max * rb, D_MODEL), BF16),
        compiler_params=_params(("arbitrary",), 32),
        name="moe_gather",
    )(*gather_tabs, h, dest_rows)

    frozen = lambda f, ok: f * ok + (nf - 1) * (1 - ok)
    y_sorted = pl.pallas_call(
        _moe_group_kernel,
        grid_spec=pltpu.PrefetchScalarGridSpec(
            num_scalar_prefetch=2,
            grid=(nb_max, nf),
            in_specs=[
                pl.BlockSpec((rb, D_MODEL), lambda b, f, e, ok: (b, 0)),
                pl.BlockSpec((None, D_MODEL, tf), lambda b, f, e, ok: (e[b], 0, frozen(f, ok[b]))),
                pl.BlockSpec((None, D_MODEL, tf), lambda b, f, e, ok: (e[b], 0, frozen(f, ok[b]) + nf)),
                pl.BlockSpec((None, tf, D_MODEL), lambda b, f, e, ok: (e[b], frozen(f, ok[b]), 0)),
            ],
            out_specs=pl.BlockSpec((rb, D_MODEL), lambda b, f, e, ok: (b, 0)),
            scratch_shapes=[pltpu.VMEM((rb, D_MODEL), F32)],
        ),
        out_shape=jax.ShapeDtypeStruct((nb_max * rb, D_MODEL), BF16),
        compiler_params=_params(("parallel", "arbitrary"), 48),
        name="moe_experts",
    )(blk_e, blk_ok, h_sorted, w_gu, w_gu, w_dn)

    tok = lambda i, t, e, blk, fl: (t[i], 0)
    return pl.pallas_call(
        _moe_combine_kernel,
        grid_spec=pltpu.PrefetchScalarGridSpec(
            num_scalar_prefetch=4,
            grid=(combine_tabs[0].shape[0],),
            in_specs=[
                pl.BlockSpec((tb, D_MODEL), tok),
                pl.BlockSpec((tb, LANES), tok),
                pl.BlockSpec((tb, LANES), tok),
                pl.BlockSpec((MOE_WIN, D_MODEL), lambda i, t, e, blk, fl: (blk[i], 0)),
            ],
            out_specs=pl.BlockSpec((tb, D_MODEL), tok),
            scratch_shapes=[pltpu.VMEM((tb, D_MODEL), F32)],
        ),
        out_shape=jax.ShapeDtypeStruct((n, D_MODEL), F32),
        compiler_params=_params(("arbitrary",), 32),
        name="moe_combine",
    )(*combine_tabs, x, comb, dest_lanes, y_sorted)


def _row(a):
    return a.reshape(1, -1).astype(F32)


def kernel(x_prompt, x_sample, cache_k, cache_v, state_hgrn, page_table, ln1_g, w_in, q_norm_g, k_norm_g,
           lam_p, subln_g, hgrn_lb, hgrn_norm_g, w_out, ln2_g, w_ffn_gu, w_ffn_dn, w_router, w_exp_gu,
           w_exp_dn):
    bp, tp, _ = x_prompt.shape
    bs = x_sample.shape[0]
    depth = w_in.shape[0]
    n_pool, page_size = cache_k.shape[1], cache_k.shape[2]
    past = page_table.shape[1] * page_size

    xp = x_prompt.reshape(bp * tp, D_MODEL)
    xs = x_sample.reshape(bs, D_MODEL)
    ck = cache_k.reshape(depth, n_pool, page_size * N_ATT_HEADS, V_DIM)
    cv = cache_v.reshape(depth, n_pool, page_size * N_ATT_HEADS, V_DIM)

    tabs_p = _rope_tables(0, tp)
    tabs_s = [jnp.broadcast_to(tab, (bs, LANES)) for tab in _rope_tables(past, 1)]
    lane = jnp.arange(LANES)
    bd = jnp.where((lane[:, None] // QK_DIM) == (lane[None, :] // QK_DIM), 1.0 / QK_DIM, 0.0).astype(BF16)
    tok = jnp.arange(HGRN_CHUNK)
    tri = (tok[None, :] <= tok[:, None]).astype(BF16)

    lbs = jnp.cumsum(jax.nn.softmax(hgrn_lb.astype(F32), axis=0), axis=0)
    lbs = lbs - lbs[0]

    n_p = bp * tp
    tm_p = min(256, n_p)
    tq = min(512, tp)
    tm_ffn = min(1024, n_p)
    outs = {name: [] for name in ("kp", "vp", "sp", "ks", "vs", "ss")}
    for l in range(depth):
        lam_init = 0.8 - 0.6 * math.exp(-0.3 * l)
        lp = lam_p[l].astype(F32)
        lam = (jnp.exp(jnp.sum(lp[0] * lp[1])) - jnp.exp(jnp.sum(lp[2] * lp[3])) + lam_init).reshape(1)
        w_in_bf = w_in[l].astype(BF16)
        w_out_bf = w_out[l].astype(BF16)
        qg, kg = _row(q_norm_g[l]), _row(k_norm_g[l])
        lb_rows = jnp.stack([jnp.log(lbs[l]), jnp.log1p(-lbs[l]), 1.0 - lbs[l]])
        sub_g, rnn_g = _row(subln_g[l]), _row(hgrn_norm_g[l])
        ln1, ln2 = _row(ln1_g[l]), _row(ln2_g[l])
        router = None
        if l % 2 == 1:
            router = jnp.pad(w_router[l // 2].astype(F32), ((0, 0), (0, LANES - N_EXPERTS)))

        q, k, v, r, k_bf, v_bf = _input_projection(xp, ln1, w_in_bf, qg, kg, bd, tabs_p, tm_p)
        att = _attention_prompt(q, k_bf, v_bf, lam, sub_g, lam_init, bp, tp, tq)
        rnn, s_fin = _hgrn_prompt(r, lb_rows, rnn_g, tri, bp, tp)
        proj_p = _output_projection(xp, att, rnn, w_out_bf, ln2, router, tm_p)
        outs["kp"].append(k.reshape(bp, tp, N_ATT_HEADS, V_DIM))
        outs["vp"].append(v.reshape(bp, tp, N_ATT_HEADS, V_DIM))
        outs["sp"].append(s_fin)

        q, k, v, r, _, _ = _input_projection(xs, ln1, w_in[l], qg, kg, bd, tabs_s, bs)
        att = _attention_decode(q, k, v, ck, cv, page_table, l, lam, sub_g, lam_init)
        rnn, s_new = _hgrn_decode(r, state_hgrn, l, lb_rows, rnn_g)
        proj_s = _output_projection(xs, att, rnn, w_out[l], ln2, router, bs)
        outs["ks"].append(k.reshape(bs, 1, N_ATT_HEADS, V_DIM))
        outs["vs"].append(v.reshape(bs, 1, N_ATT_HEADS, V_DIM))
        outs["ss"].append(s_new)

        if l % 2 == 0:
            w_gu = w_ffn_gu[l // 2].astype(BF16)
            w_dn = w_ffn_dn[l // 2].astype(BF16)
            xp = _dense_ffn(proj_p[0], proj_p[1], w_gu, w_dn, tm_ffn, 1408)
            xs = _dense_ffn(proj_s[0], proj_s[1], w_ffn_gu[l // 2], w_ffn_dn[l // 2], bs, 256)
        else:
            w_gu = w_exp_gu[l // 2].astype(BF16)
            w_dn = w_exp_dn[l // 2].astype(BF16)
            xp = _moe_routed(proj_p[0], proj_p[1], proj_p[2], w_gu, w_dn, 896)
            xs = _moe_ffn(proj_s[0], proj_s[1].astype(BF16), proj_s[2], w_gu, w_dn, bs, 896)

    return (xp.reshape(bp, tp, D_MODEL), xs.reshape(bs, 1, D_MODEL),
            jnp.stack(outs["kp"]), jnp.stack(outs["vp"]), jnp.stack(outs["sp"]),
            jnp.stack(outs["ks"]), jnp.stack(outs["vs"]), jnp.stack(outs["ss"]))
```
